```python
import jax, jax.numpy as jnp
from jax import lax
import numpy as np

D_MODEL = 2048
BATCH = 1
SEQ = 8192
DEPTH = 2

GRID_W = 64
PLE_DIM = 256
HG_HEADS = 8
HG_DK = 128
HG_DV = 128
HG_W = HG_HEADS * HG_DK
HG_CHUNK = 64
NA_HEADS = 8
NA_HD = 128
NA_W = NA_HEADS * NA_HD
NA_KH = 8
NA_KW = 16
N_IN = 5 * HG_W + 4 * NA_W + 2 * D_MODEL
EPS = 1e-6
NEG = -1e30
F_MIN = 1e-6

kernel_name = "hybrid_hgrn2_natten_gated_encoder"

_SPLITS = [HG_W, 2 * HG_W, 3 * HG_W, 4 * HG_W, 5 * HG_W,
           5 * HG_W + NA_W, 5 * HG_W + 2 * NA_W, 5 * HG_W + 3 * NA_W,
           5 * HG_W + 4 * NA_W, 5 * HG_W + 4 * NA_W + D_MODEL]


def _rms(x, g):
    xf = x.astype(jnp.float32)
    y = xf * lax.rsqrt(jnp.mean(xf * xf, axis=-1, keepdims=True) + EPS)
    return (y * g.astype(jnp.float32)).astype(x.dtype)


def _gla_scan(q, k, v, logf):
    B, S, H, dk = q.shape
    C = HG_CHUNK
    n = S // C

    def to_chunks(a):
        return a.astype(jnp.float32).reshape(B, n, C, H, a.shape[-1]).transpose(1, 0, 3, 2, 4)

    qc, kc, vc, gc = to_chunks(q), to_chunks(k), to_chunks(v), to_chunks(logf)
    tri = jnp.tril(jnp.ones((C, C), dtype=bool))

    def step(state, xs):
        qb, kb, vb, gb = xs
        b = jnp.cumsum(gb, axis=-2)
        o_inter = jnp.einsum('bhtk,bhkv->bhtv', qb * jnp.exp(b), state)
        diff = b[:, :, :, None, :] - b[:, :, None, :, :]
        decay = jnp.exp(jnp.where(tri[:, :, None], diff, NEG))
        att = jnp.einsum('bhtk,bhsk,bhtsk->bhts', qb, kb, decay)
        o_intra = jnp.einsum('bhts,bhsv->bhtv', att, vb)
        b_last = b[:, :, -1:, :]
        new_state = state * jnp.exp(b_last[:, :, 0, :, None]) + jnp.einsum(
            'bhsk,bhsv->bhkv', kb * jnp.exp(b_last - b), vb)
        return new_state, o_inter + o_intra

    s0 = jnp.zeros((B, H, dk, v.shape[-1]), jnp.float32)
    _, out = lax.scan(step, s0, (qc, kc, vc, gc))
    return out.transpose(1, 0, 3, 2, 4).reshape(B, S, H, v.shape[-1])


def _hgrn2_bidir(q_raw, i_raw, z_fwd, z_bwd, lb_fwd, lb_bwd):
    B, S, _ = q_raw.shape
    q = jax.nn.silu(q_raw).reshape(B, S, HG_HEADS, HG_DK)
    v = i_raw.reshape(B, S, HG_HEADS, HG_DV)

    def gates(z, lb):
        z = z.astype(jnp.float32).reshape(B, S, HG_HEADS, HG_DK)
        lb = lb.reshape(HG_HEADS, HG_DK)
        f = lb + (1.0 - lb) * jax.nn.sigmoid(z)
        logf = jnp.log(jnp.maximum(f, F_MIN))
        k = (1.0 - lb) * jax.nn.sigmoid(-z)
        return k, logf

    kf, gf = gates(z_fwd, lb_fwd)
    kb, gb = gates(z_bwd, lb_bwd)
    o_f = _gla_scan(q, kf, v, gf)
    rev = lambda a: jnp.flip(a, axis=1)
    o_b = rev(_gla_scan(rev(q), rev(kb), rev(v), rev(gb)))
    return o_f + o_b


def _neighbourhood_attention(q, k, v, rpb):
    B, S, H, hd = q.shape
    rows = S // GRID_W
    kh = min(NA_KH, rows)
    kw = NA_KW
    r = jnp.arange(rows)
    r_start = jnp.clip(r - kh // 2, 0, rows - kh)
    ridx = r_start[:, None] + jnp.arange(kh)[None, :]
    c = jnp.arange(GRID_W)
    c_start = jnp.clip(c - kw // 2, 0, GRID_W - kw)
    col_in = (c[None, :] >= c_start[:, None]) & (c[None, :] < c_start[:, None] + kw)

    qg = q.reshape(B, rows, GRID_W, H, hd)
    kg = k.reshape(B, rows, GRID_W, H, hd)[:, ridx]
    vg = v.reshape(B, rows, GRID_W, H, hd)[:, ridx]

    s = jnp.einsum('brqhd,brakhd->brhqak', qg, kg).astype(jnp.float32) * (hd ** -0.5)
    row_off = ridx - r[:, None] + (NA_KH - 1)
    col_off = jnp.clip(c[None, :] - c[:, None] + (NA_KW - 1), 0, 2 * NA_KW - 2)
    bias = rpb.astype(jnp.float32)[:, row_off[:, None, :, None], col_off[None, :, None, :]]
    bias = bias.transpose(1, 0, 2, 3, 4)
    s = jnp.where(col_in[None, None, None, :, None, :], s + bias[None], NEG)
    pr = jax.nn.softmax(s.reshape(B, rows, H, GRID_W, kh * GRID_W), axis=-1).reshape(s.shape)
    o = jnp.einsum('brhqak,brakhd->brqhd', pr.astype(v.dtype), vg)
    return o.reshape(B, S, H * hd)


def setup_inputs(seed: int = 0) -> dict:
    key = jax.random.key(seed)
    ks = jax.random.split(key, 16)
    n = jax.random.normal
    f32 = jnp.float32
    return {
        "x": n(ks[0], (BATCH, SEQ, D_MODEL), f32),
        "p": n(ks[1], (DEPTH, BATCH, SEQ, PLE_DIM), f32),
        "norm_g": 1.0 + 0.02 * n(ks[2], (DEPTH, D_MODEL), f32),
        "w_in": n(ks[3], (DEPTH, D_MODEL, N_IN), f32) * D_MODEL ** -0.5,
        "hgrn_lb": n(ks[4], (DEPTH, 2, HG_W), f32),
        "hgrn_onorm": 1.0 + 0.02 * n(ks[5], (DEPTH, HG_DV), f32),
        "na_qnorm": 1.0 + 0.02 * n(ks[6], (DEPTH, NA_HD), f32),
        "na_knorm": 1.0 + 0.02 * n(ks[7], (DEPTH, NA_HD), f32),
        "na_rpb": 0.02 * n(ks[8], (DEPTH, NA_HEADS, 2 * NA_KH - 1, 2 * NA_KW - 1), f32),
        "w_branch_a": n(ks[9], (DEPTH, HG_W, D_MODEL), f32) * HG_W ** -0.5,
        "w_branch_b": n(ks[10], (DEPTH, NA_W, D_MODEL), f32) * NA_W ** -0.5,
        "w_out": n(ks[11], (DEPTH, D_MODEL, D_MODEL), f32) * D_MODEL ** -0.5,
        "ple_norm": 1.0 + 0.02 * n(ks[12], (DEPTH, D_MODEL), f32),
        "w_ple_gate": n(ks[13], (DEPTH, D_MODEL, D_MODEL), f32) * D_MODEL ** -0.5,
        "w_ple": n(ks[14], (DEPTH, PLE_DIM, D_MODEL), f32) * PLE_DIM ** -0.5,
    }


def reference(x, p, norm_g, w_in, hgrn_lb, hgrn_onorm, na_qnorm, na_knorm, na_rpb,
              w_branch_a, w_branch_b, w_out, ple_norm, w_ple_gate, w_ple):
    B, S, _ = x.shape
    lbp = jax.nn.softmax(hgrn_lb.astype(jnp.float32), axis=0)
    lower = jnp.cumsum(lbp, axis=0) - lbp[0:1]

    for i in range(DEPTH):
        h = _rms(x, norm_g[i])
        u = h @ w_in[i]
        (hq, hi, hf_fwd, hf_bwd, hgate, nq, nk, nv, ngate, m_a, m_b) = jnp.split(u, _SPLITS, axis=-1)

        oa = _hgrn2_bidir(hq, hi, hf_fwd, hf_bwd, lower[i, 0], lower[i, 1])
        oa = _rms(oa, hgrn_onorm[i]).reshape(B, S, HG_W).astype(x.dtype)
        ya = (oa * jax.nn.silu(hgate)) @ w_branch_a[i]

        q = _rms(nq.reshape(B, S, NA_HEADS, NA_HD), na_qnorm[i])
        k = _rms(nk.reshape(B, S, NA_HEADS, NA_HD), na_knorm[i])
        v = nv.reshape(B, S, NA_HEADS, NA_HD)
        ob = _neighbourhood_attention(q, k, v, na_rpb[i])
        yb = (ob * jax.nn.silu(ngate)) @ w_branch_b[i]

        y = jax.nn.sigmoid(m_a) * ya + jax.nn.sigmoid(m_b) * yb
        x = x + y @ w_out[i]

        g = jax.nn.sigmoid(_rms(x, ple_norm[i]) @ w_ple_gate[i])
        x = x + (p[i] @ w_ple[i]) * g
    return x
```

```python
import functools

import jax
import jax.numpy as jnp
import numpy as np
from jax import lax
from jax.experimental import pallas as pl
from jax.experimental.pallas import tpu as pltpu

F32 = jnp.float32
BF16 = jnp.bfloat16

GRID_W = 64
HG_HEADS = 8
HG_DK = 128
HG_W = HG_HEADS * HG_DK
NA_HEADS = 8
NA_HD = 128
NA_W = NA_HEADS * NA_HD
NA_KH = 8
NA_KW = 16
EPS = 1e-6
NEG = -1e30
F_MIN = 1e-6

CHUNK = 64
SUB = 8
NSUB = CHUNK // SUB
MID = SUB // 2

NA_ROWS_PER_STEP = 4
NA_WIN_ROWS = 12

VMEM_LIMIT = 56 * 1024 * 1024


def _dot(a, b):
    return jnp.dot(a, b, preferred_element_type=F32)


def _dot_nt(a, b):
    return lax.dot_general(a, b, (((1,), (1,)), ((), ())), preferred_element_type=F32)


def _dot_tn(a, b):
    return lax.dot_general(a, b, (((0,), (0,)), ((), ())), preferred_element_type=F32)


def _sigmoid(x):
    return 1.0 / (1.0 + jnp.exp(-x))


def _silu(x):
    return x * _sigmoid(x)


def _inproj_kernel(x_ref, g_ref, w_ref, o_ref, h_ref):
    @pl.when(pl.program_id(1) == 0)
    def _():
        x = x_ref[...]
        ms = jnp.mean(x * x, axis=-1, keepdims=True)
        h_ref[...] = (x * lax.rsqrt(ms + EPS) * g_ref[...]).astype(BF16)

    o_ref[...] = _dot(h_ref[...], w_ref[...]).astype(o_ref.dtype)


def _in_proj(x, g, w, tm=1024, tn=1024):
    s, d = x.shape
    n = w.shape[1]
    tm = min(tm, s)
    return pl.pallas_call(
        _inproj_kernel,
        grid=(s // tm, n // tn),
        in_specs=[
            pl.BlockSpec((tm, d), lambda i, j: (i, 0)),
            pl.BlockSpec((1, d), lambda i, j: (0, 0)),
            pl.BlockSpec((d, tn), lambda i, j: (0, j)),
        ],
        out_specs=pl.BlockSpec((tm, tn), lambda i, j: (i, j)),
        out_shape=jax.ShapeDtypeStruct((s, n), BF16),
        scratch_shapes=[pltpu.VMEM((tm, d), BF16)],
        compiler_params=pltpu.CompilerParams(
            dimension_semantics=("parallel", "arbitrary"),
            vmem_limit_bytes=VMEM_LIMIT),
        name="in_proj",
    )(x, g.reshape(1, d), w)


def _hgrn_direction(q_ref, v_ref, z_ref, lb, o_ref, st_ref, backward):
    c = CHUNK
    w = q_ref.shape[1]
    q = q_ref[...].astype(F32)
    v = v_ref[...]
    z = z_ref[...].astype(F32)

    sg = _sigmoid(z)
    f = lb + (1.0 - lb) * sg
    logf = jnp.log(jnp.maximum(f, F_MIN))
    k = (1.0 - lb) * _sigmoid(-z)
    qs = q * _sigmoid(q)

    row = lax.broadcasted_iota(jnp.int32, (c, c), 0)
    col = lax.broadcasted_iota(jnp.int32, (c, c), 1)
    tri = (row <= col) if backward else (row >= col)
    tri_b = jnp.where(tri, 1.0, 0.0).astype(BF16)
    hi = logf.astype(BF16)
    lo = (logf - hi.astype(F32)).astype(BF16)
    b = _dot(tri_b, hi) + _dot(tri_b, lo)

    total = b[0:1, :] if backward else b[c - 1:c, :]
    q_hat = (qs * jnp.exp(b)).astype(BF16)
    k_hat = (k * jnp.exp(total - b)).astype(BF16)
    dec = jnp.exp(total)

    blocks = [slice(j * SUB, (j + 1) * SUB) for j in range(NSUB)]
    mids = [b[j * SUB + MID:j * SUB + MID + 1, :] for j in range(NSUB)]
    q0 = [qs[blocks[j]] * jnp.exp(b[blocks[j]] - mids[j]) for j in range(NSUB)]
    k0 = [(k[blocks[j]] * jnp.exp(mids[j] - b[blocks[j]])).astype(BF16) for j in range(NSUB)]
    zero_q = jnp.zeros((SUB, w), BF16)

    order = range(NSUB) if backward else range(NSUB - 1, -1, -1)
    cur = [None] * NSUB
    q_tilde = [None] * NSUB
    k_tilde = [None] * NSUB
    prev = None
    for j in order:
        if prev is not None:
            step = jnp.exp(mids[prev] - mids[j])
            for jj in range(NSUB):
                if cur[jj] is not None:
                    cur[jj] = cur[jj] * step
        cur[j] = q0[j]
        q_tilde[j] = jnp.concatenate(
            [zero_q if cur[jj] is None else cur[jj].astype(BF16) for jj in range(NSUB)], axis=0)
        k_tilde[j] = jnp.concatenate(
            [k0[jj] if jj == j else zero_q for jj in range(NSUB)], axis=0)
        prev = j

    causal = (row >= col) if backward else (row <= col)

    for h in range(w // HG_DK):
        hs = slice(h * HG_DK, (h + 1) * HG_DK)
        st = st_ref[h]
        o_inter = _dot_nt(q_hat[:, hs], st.astype(BF16))
        att_t = _dot_nt(k_tilde[0][:, hs], q_tilde[0][:, hs])
        for j in range(1, NSUB):
            att_t = att_t + _dot_nt(k_tilde[j][:, hs], q_tilde[j][:, hs])
        att_t = jnp.where(causal, att_t, 0.0).astype(BF16)
        vh = v[:, hs]
        o_intra = _dot_tn(att_t, vh)
        o_ref[:, hs] = (o_inter + o_intra).astype(o_ref.dtype)
        st_ref[h] = st * dec[:, hs] + _dot_tn(vh, k_hat[:, hs])


def _hgrn_kernel(qf_ref, vf_ref, zf_ref, qb_ref, vb_ref, zb_ref, lb_ref,
                 of_ref, ob_ref, stf_ref, stb_ref):
    @pl.when(pl.program_id(0) == 0)
    def _():
        stf_ref[...] = jnp.zeros_like(stf_ref)
        stb_ref[...] = jnp.zeros_like(stb_ref)

    _hgrn_direction(qf_ref, vf_ref, zf_ref, lb_ref[0:1, :], of_ref, stf_ref, False)
    _hgrn_direction(qb_ref, vb_ref, zb_ref, lb_ref[1:2, :], ob_ref, stb_ref, True)


def _hgrn(u, lb):
    s = u.shape[0]
    n = s // CHUNK
    fwd = lambda cb: pl.BlockSpec((CHUNK, HG_W), lambda i, cb=cb: (i, cb))
    bwd = lambda cb: pl.BlockSpec((CHUNK, HG_W), lambda i, cb=cb: (n - 1 - i, cb))
    heads = HG_W // HG_DK
    return pl.pallas_call(
        _hgrn_kernel,
        grid=(n,),
        in_specs=[fwd(0), fwd(1), fwd(2), bwd(0), bwd(1), bwd(3),
                  pl.BlockSpec((2, HG_W), lambda i: (0, 0))],
        out_specs=[pl.BlockSpec((CHUNK, HG_W), lambda i: (i, 0)),
                   pl.BlockSpec((CHUNK, HG_W), lambda i: (n - 1 - i, 0))],
        out_shape=[jax.ShapeDtypeStruct((s, HG_W), BF16)] * 2,
        scratch_shapes=[pltpu.VMEM((heads, HG_DK, HG_DK), F32)] * 2,
        compiler_params=pltpu.CompilerParams(
            dimension_semantics=("arbitrary",),
            vmem_limit_bytes=VMEM_LIMIT),
        name="hgrn",
    )(u, u, u, u, u, u, lb)


def _natten_kernel(q_ref, k_ref, v_ref, qg_ref, kg_ref, bias_ref, o_ref, kn_ref, *, n_steps):
    rb = pl.program_id(1)
    s = k_ref.shape[0]
    kblk = 512

    @pl.when(rb == 0)
    def _():
        def body(i, carry):
            sl = pl.ds(pl.multiple_of(i * kblk, kblk), kblk)
            kk = k_ref[sl, :].astype(F32)
            ms = jnp.mean(kk * kk, axis=-1, keepdims=True)
            kn_ref[sl, :] = (kk * lax.rsqrt(ms + EPS) * kg_ref[...]).astype(BF16)
            return carry
        lax.fori_loop(0, s // kblk, body, 0)

    q = q_ref[...].astype(F32)
    ms = jnp.mean(q * q, axis=-1, keepdims=True)
    qn = (q * lax.rsqrt(ms + EPS) * (qg_ref[...] * (NA_HD ** -0.5))).astype(BF16)

    rows = s // GRID_W
    w0 = jnp.clip(NA_ROWS_PER_STEP * rb - NA_KH // 2, 0, rows - NA_WIN_ROWS) * GRID_W
    w0 = pl.multiple_of(w0, GRID_W)
    win = NA_WIN_ROWS * GRID_W
    kw = kn_ref[pl.ds(w0, win), :]
    vw = v_ref[pl.ds(w0, win), :]
    sc = _dot_nt(qn, kw) + bias_ref[0, 0]
    m = jnp.max(sc, axis=-1, keepdims=True)
    p = jnp.exp(sc - m)
    l = jnp.sum(p, axis=-1, keepdims=True)
    o = _dot(p.astype(BF16), vw) / l
    o_ref[...] = o.astype(o_ref.dtype)


def _natten_bias(rpb, rows):
    rq, rk = NA_ROWS_PER_STEP, NA_WIN_ROWS
    n_steps = rows // rq
    c = np.arange(GRID_W)
    c_start = np.clip(c - NA_KW // 2, 0, GRID_W - NA_KW)
    col_in = (c[None, :] >= c_start[:, None]) & (c[None, :] < c_start[:, None] + NA_KW)
    col_off = np.clip(c[None, :] - c[:, None] + (NA_KW - 1), 0, 2 * NA_KW - 2)
    kh = min(NA_KH, rows)
    tabs_row, tabs_ok = [], []
    for rb in (0, min(1, n_steps - 1), n_steps - 1):
        w0 = int(np.clip(rq * rb - NA_KH // 2, 0, rows - rk))
        r = rq * rb + np.arange(rq)
        a = w0 + np.arange(rk)
        r_start = np.clip(r - kh // 2, 0, rows - kh)
        ok = (a[None, :] >= r_start[:, None]) & (a[None, :] < r_start[:, None] + kh)
        row_off = np.clip(a[None, :] - r[:, None] + (NA_KH - 1), 0, 2 * NA_KH - 2)
        tabs_row.append(row_off)
        tabs_ok.append(ok)
    row_off = np.stack(tabs_row)
    ok = np.stack(tabs_ok)
    ro = row_off[:, :, None, :, None]
    co = col_off[None, None, :, None, :]
    mask = ok[:, :, None, :, None] & col_in[None, None, :, None, :]
    ro, co, mask = np.broadcast_arrays(ro, co, mask)
    bias = rpb.astype(F32)[:, ro, co]
    bias = jnp.where(mask[None], bias, NEG)
    return bias.reshape(rpb.shape[0], 3, rq * GRID_W, rk * GRID_W)


def _natten(u, qg, kg, bias, col0):
    s = u.shape[0]
    rows = s // GRID_W
    tq = NA_ROWS_PER_STEP * GRID_W
    n_steps = s // tq

    def var(rb):
        return jnp.where(rb == 0, 0, jnp.where(rb == n_steps - 1, 2, 1))

    return pl.pallas_call(
        functools.partial(_natten_kernel, n_steps=n_steps),
        grid=(NA_HEADS, n_steps),
        in_specs=[
            pl.BlockSpec((tq, NA_HD), lambda h, rb: (rb, col0 + h)),
            pl.BlockSpec((s, NA_HD), lambda h, rb: (0, col0 + NA_HEADS + h)),
            pl.BlockSpec((s, NA_HD), lambda h, rb: (0, col0 + 2 * NA_HEADS + h)),
            pl.BlockSpec((1, NA_HD), lambda h, rb: (0, 0)),
            pl.BlockSpec((1, NA_HD), lambda h, rb: (0, 0)),
            pl.BlockSpec((1, 1, tq, NA_WIN_ROWS * GRID_W), lambda h, rb: (h, var(rb), 0, 0)),
        ],
        out_specs=pl.BlockSpec((tq, NA_HD), lambda h, rb: (rb, h)),
        out_shape=jax.ShapeDtypeStruct((s, NA_W), BF16),
        scratch_shapes=[pltpu.VMEM((s, NA_HD), BF16)],
        compiler_params=pltpu.CompilerParams(
            dimension_semantics=("parallel", "arbitrary"),
            vmem_limit_bytes=VMEM_LIMIT),
        name="natten",
    )(u, u, u, qg.reshape(1, NA_HD), kg.reshape(1, NA_HD), bias)


def _merge_kernel(of_ref, ob_ref, hg_ref, nb_ref, ng_ref, ma0_ref, ma1_ref, mb0_ref, mb1_ref,
                  x_ref, p_ref, on_ref, pn_ref, wa_ref, wb_ref, wo_ref, wg_ref, wp_ref, o_ref):
    oa = of_ref[...].astype(F32) + ob_ref[...].astype(F32)
    gate_a = _silu(hg_ref[...].astype(F32))
    parts = []
    for h in range(HG_HEADS):
        hs = slice(h * HG_DK, (h + 1) * HG_DK)
        oh = oa[:, hs]
        ms = jnp.mean(oh * oh, axis=-1, keepdims=True)
        parts.append((oh * lax.rsqrt(ms + EPS) * on_ref[...] * gate_a[:, hs]).astype(BF16))
    a_in = jnp.concatenate(parts, axis=1)
    ya = _dot(a_in, wa_ref[...])
    b_in = (nb_ref[...].astype(F32) * _silu(ng_ref[...].astype(F32))).astype(BF16)
    yb = _dot(b_in, wb_ref[...])
    m_a = jnp.concatenate([ma0_ref[...], ma1_ref[...]], axis=1).astype(F32)
    m_b = jnp.concatenate([mb0_ref[...], mb1_ref[...]], axis=1).astype(F32)
    y = _sigmoid(m_a) * ya + _sigmoid(m_b) * yb
    x1 = x_ref[...] + _dot(y.astype(BF16), wo_ref[...])
    ms = jnp.mean(x1 * x1, axis=-1, keepdims=True)
    hn = (x1 * lax.rsqrt(ms + EPS) * pn_ref[...]).astype(BF16)
    g = _sigmoid(_dot(hn, wg_ref[...]))
    o_ref[...] = x1 + _dot(p_ref[...].astype(BF16), wp_ref[...]) * g


def _merge(o_f, o_b, u, nb, x, p, onorm, pnorm, wa, wb, wo, wg, wp, cb_hgate, cb_ngate, cb_ma, tm=256):
    s, d = x.shape
    tm = min(tm, s)
    pd = p.shape[1]
    act = lambda wdt: pl.BlockSpec((tm, wdt), lambda i: (i, 0))
    ucol = lambda cb: pl.BlockSpec((tm, HG_W), lambda i, cb=cb: (i, cb))
    const = lambda shp: pl.BlockSpec(shp, lambda i: (0, 0), pipeline_mode=pl.Buffered(1))
    return pl.pallas_call(
        _merge_kernel,
        grid=(s // tm,),
        in_specs=[act(HG_W), act(HG_W), ucol(cb_hgate), act(NA_W), ucol(cb_ngate),
                  ucol(cb_ma), ucol(cb_ma + 1), ucol(cb_ma + 2), ucol(cb_ma + 3),
                  act(d), act(pd),
                  const((1, HG_DK)), const((1, d)),
                  const(wa.shape), const(wb.shape), const(wo.shape), const(wg.shape), const(wp.shape)],
        out_specs=act(d),
        out_shape=jax.ShapeDtypeStruct((s, d), F32),
        compiler_params=pltpu.CompilerParams(
            dimension_semantics=("parallel",),
            vmem_limit_bytes=VMEM_LIMIT),
        name="merge",
    )(o_f, o_b, u, nb, u, u, u, u, u, x, p, onorm.reshape(1, HG_DK), pnorm.reshape(1, d),
      wa, wb, wo, wg, wp)


def kernel(x, p, norm_g, w_in, hgrn_lb, hgrn_onorm, na_qnorm, na_knorm, na_rpb,
           w_branch_a, w_branch_b, w_out, ple_norm, w_ple_gate, w_ple):
    bsz, s, d = x.shape
    depth = w_in.shape[0]
    rows = s // GRID_W
    lbp = jax.nn.softmax(hgrn_lb.astype(F32), axis=0)
    lower = jnp.cumsum(lbp, axis=0) - lbp[0:1]

    outs = []
    for bi in range(bsz):
        xb = x[bi]
        for i in range(depth):
            u = _in_proj(xb, norm_g[i], w_in[i].astype(BF16))
            o_f, o_b = _hgrn(u, lower[i])
            bias = _natten_bias(na_rpb[i], rows)
            nb = _natten(u, na_qnorm[i], na_knorm[i], bias, col0=5 * HG_W // NA_HD)
            xb = _merge(o_f, o_b, u, nb, xb, p[i, bi], hgrn_onorm[i], ple_norm[i],
                        w_branch_a[i].astype(BF16), w_branch_b[i].astype(BF16),
                        w_out[i].astype(BF16), w_ple_gate[i].astype(BF16), w_ple[i].astype(BF16),
                        cb_hgate=4, cb_ngate=8, cb_ma=9)
        outs.append(xb)
    return jnp.stack(outs, axis=0)
```

```python
import functools

import jax
import jax.numpy as jnp
import numpy as np
from jax import lax
from jax.experimental import pallas as pl
from jax.experimental.pallas import tpu as pltpu

F32 = jnp.float32
BF16 = jnp.bfloat16

GRID_W = 64
HG_HEADS = 8
HG_DK = 128
HG_W = HG_HEADS * HG_DK
NA_HEADS = 8
NA_HD = 128
NA_W = NA_HEADS * NA_HD
NA_KH = 8
NA_KW = 16
EPS = 1e-6
NEG = -1e30
F_MIN = 1e-6

CHUNK = 64
SUB = 8
NSUB = CHUNK // SUB
MID = SUB // 2

NA_ROWS_PER_STEP = 4
NA_WIN_ROWS = 12

VMEM_LIMIT = 56 * 1024 * 1024


def _dot(a, b):
    return jnp.dot(a, b, preferred_element_type=F32)


def _dot_nt(a, b):
    return lax.dot_general(a, b, (((1,), (1,)), ((), ())), preferred_element_type=F32)


def _dot_tn(a, b):
    return lax.dot_general(a, b, (((0,), (0,)), ((), ())), preferred_element_type=F32)


def _sigmoid(x):
    return 1.0 / (1.0 + jnp.exp(-x))


def _silu(x):
    return x * _sigmoid(x)


def _inproj_kernel(x_ref, g_ref, w_ref, o_ref, h_ref):
    @pl.when(pl.program_id(1) == 0)
    def _():
        x = x_ref[...]
        ms = jnp.mean(x * x, axis=-1, keepdims=True)
        h_ref[...] = (x * lax.rsqrt(ms + EPS) * g_ref[...]).astype(BF16)

    o_ref[...] = _dot(h_ref[...], w_ref[...]).astype(o_ref.dtype)


def _in_proj(x, g, w, tm=1024, tn=1024):
    s, d = x.shape
    n = w.shape[1]
    tm = min(tm, s)
    return pl.pallas_call(
        _inproj_kernel,
        grid=(s // tm, n // tn),
        in_specs=[
            pl.BlockSpec((tm, d), lambda i, j: (i, 0)),
            pl.BlockSpec((1, d), lambda i, j: (0, 0)),
            pl.BlockSpec((d, tn), lambda i, j: (0, j)),
        ],
        out_specs=pl.BlockSpec((tm, tn), lambda i, j: (i, j)),
        out_shape=jax.ShapeDtypeStruct((s, n), BF16),
        scratch_shapes=[pltpu.VMEM((tm, d), BF16)],
        compiler_params=pltpu.CompilerParams(
            dimension_semantics=("parallel", "arbitrary"),
            vmem_limit_bytes=VMEM_LIMIT),
        name="in_proj",
    )(x, g.reshape(1, d), w)


def _hgrn_selectors():
    c = CHUNK
    t = np.arange(c)[:, None]
    r = np.arange(c)[None, :]
    m = (t // SUB) * SUB + MID
    out = []
    for backward in (False, True):
        if backward:
            cum = lambda p: (r >= p).astype(np.float32)
            prev_m = m - SUB
        else:
            cum = lambda p: (r <= p).astype(np.float32)
            prev_m = m + SUB
        has_prev = (prev_m >= 0) & (prev_m < c)
        step = np.where(has_prev, cum(np.clip(prev_m, 0, c - 1)) - cum(m), 0.0)
        total = np.ones((16, c), np.float32)
        sel = np.concatenate([cum(t) - cum(m), cum(m), 1.0 - cum(m), step, total], axis=0)
        out.append(np.concatenate([sel, sel], axis=1))
    return np.stack(out)


HG_SEL_ROWS = 4 * CHUNK + 16


def _hgrn_prologue(q_ref, v_ref, z_ref, lb, sel, backward):
    c = CHUNK
    w = q_ref.shape[1]
    q = q_ref[...].astype(F32)
    v = v_ref[...]
    z = z_ref[...].astype(F32)

    f = lb + (1.0 - lb) * _sigmoid(z)
    g = jnp.log2(jnp.maximum(f, F_MIN))
    k = 1.0 - f
    qs = q * _sigmoid(q)

    hi = g.astype(BF16)
    lo = (g - hi.astype(F32)).astype(BF16)
    cums = _dot(sel, jnp.concatenate([hi, lo], axis=0))
    e = jnp.exp2(cums[0:c])
    to_ref = jnp.exp2(cums[c:2 * c])
    from_ref = jnp.exp2(cums[2 * c:3 * c])
    step = jnp.exp2(cums[3 * c:4 * c])
    dec = jnp.exp2(cums[4 * c:4 * c + SUB])

    q0 = qs * e
    k0 = k * (1.0 / e)
    q_hat = (q0 * to_ref).astype(BF16)
    k_hat = (k0 * from_ref).astype(BF16)

    blocks = [slice(j * SUB, (j + 1) * SUB) for j in range(NSUB)]
    zero_blk = jnp.zeros((SUB, w), F32)
    order = range(NSUB) if backward else range(NSUB - 1, -1, -1)
    cur = [None] * NSUB
    q_tilde = [None] * NSUB
    k_tilde = [None] * NSUB
    for j in order:
        for jj in range(NSUB):
            if cur[jj] is not None:
                cur[jj] = cur[jj] * step[blocks[j]]
        cur[j] = q0[blocks[j]]
        q_tilde[j] = jnp.concatenate(
            [zero_blk if cur[jj] is None else cur[jj] for jj in range(NSUB)], axis=0).astype(BF16)
        k_tilde[j] = jnp.concatenate(
            [k0[blocks[jj]] if jj == j else zero_blk for jj in range(NSUB)], axis=0).astype(BF16)

    row = lax.broadcasted_iota(jnp.int32, (c, c), 0)
    col = lax.broadcasted_iota(jnp.int32, (c, c), 1)
    causal = (row <= col) if backward else (row >= col)
    heads = []
    for h in range(w // HG_DK):
        hs = slice(h * HG_DK, (h + 1) * HG_DK)
        heads.append(dict(
            q_hat=q_hat[:, hs], k_hat=k_hat[:, hs], v=v[:, hs], dec=dec[:, hs],
            q_cat=jnp.concatenate([q_tilde[j][:, hs] for j in range(NSUB)], axis=1),
            k_cat=jnp.concatenate([k_tilde[j][:, hs] for j in range(NSUB)], axis=1)))
    return heads, causal


def _hgrn_kernel(qf_ref, vf_ref, zf_ref, qb_ref, vb_ref, zb_ref, lb_ref, sel_ref,
                 of_ref, ob_ref, stf_ref, stb_ref):
    @pl.when(pl.program_id(0) == 0)
    def _():
        stf_ref[...] = jnp.zeros_like(stf_ref)
        stb_ref[...] = jnp.zeros_like(stb_ref)

    heads_f, causal_f = _hgrn_prologue(qf_ref, vf_ref, zf_ref, lb_ref[0:1, :], sel_ref[0], False)
    heads_b, causal_b = _hgrn_prologue(qb_ref, vb_ref, zb_ref, lb_ref[1:2, :], sel_ref[1], True)
    n_heads = len(heads_f)
    work = ([(heads_f[h], causal_f, of_ref, stf_ref, h) for h in range(n_heads)]
            + [(heads_b[h], causal_b, ob_ref, stb_ref, h) for h in range(n_heads)])

    states = [st_ref[h] for (_, _, _, st_ref, h) in work]
    atts = [_dot_nt(a["q_cat"], a["k_cat"]) for (a, _, _, _, _) in work]
    inters = [_dot_nt(a["q_hat"], st.astype(BF16)) for (a, _, _, _, _), st in zip(work, states)]
    updates = [_dot_tn(a["v"], a["k_hat"]) for (a, _, _, _, _) in work]
    atts = [jnp.where(causal, att, 0.0).astype(BF16) for att, (_, causal, _, _, _) in zip(atts, work)]
    intras = [_dot(att, a["v"]) for att, (a, _, _, _, _) in zip(atts, work)]
    for (a, _, o_ref, st_ref, h), st, inter, intra, upd in zip(work, states, inters, intras, updates):
        o_ref[:, h * HG_DK:(h + 1) * HG_DK] = (inter + intra).astype(o_ref.dtype)
        decayed = st.reshape(HG_DK // SUB, SUB, HG_DK) * a["dec"][None]
        st_ref[h] = decayed.reshape(HG_DK, HG_DK) + upd


def _hgrn(u, lb):
    s = u.shape[0]
    n = s // CHUNK
    fwd = lambda cb: pl.BlockSpec((CHUNK, HG_W), lambda i, cb=cb: (i, cb))
    bwd = lambda cb: pl.BlockSpec((CHUNK, HG_W), lambda i, cb=cb: (n - 1 - i, cb))
    heads = HG_W // HG_DK
    return pl.pallas_call(
        _hgrn_kernel,
        grid=(n,),
        in_specs=[fwd(0), fwd(1), fwd(2), bwd(0), bwd(1), bwd(3),
                  pl.BlockSpec((2, HG_W), lambda i: (0, 0)),
                  pl.BlockSpec((2, HG_SEL_ROWS, 2 * CHUNK), lambda i: (0, 0, 0))],
        out_specs=[pl.BlockSpec((CHUNK, HG_W), lambda i: (i, 0)),
                   pl.BlockSpec((CHUNK, HG_W), lambda i: (n - 1 - i, 0))],
        out_shape=[jax.ShapeDtypeStruct((s, HG_W), BF16)] * 2,
        scratch_shapes=[pltpu.VMEM((heads, HG_DK, HG_DK), F32)] * 2,
        compiler_params=pltpu.CompilerParams(
            dimension_semantics=("arbitrary",),
            vmem_limit_bytes=VMEM_LIMIT),
        name="hgrn",
    )(u, u, u, u, u, u, lb, jnp.asarray(_hgrn_selectors(), BF16))


NA_NEG_BLOCK = 2 * NA_KH - 1


def _natten_fill_bias(tz_ref, bias_ref, w0, r0, rows):
    kh = min(NA_KH, rows)
    lane = lax.broadcasted_iota(jnp.int32, (GRID_W, 2 * GRID_W), 1)

    def block_index(qi, a):
        r = r0 + qi
        r_start = min(max(r - kh // 2, 0), rows - kh)
        ar = w0 + a
        if r_start <= ar < r_start + kh:
            return ar - r + (NA_KH - 1)
        return NA_NEG_BLOCK

    for qi in range(NA_ROWS_PER_STEP):
        for ap in range(NA_WIN_ROWS // 2):
            ie, io = block_index(qi, 2 * ap), block_index(qi, 2 * ap + 1)
            blk = tz_ref[0, ie]
            if io != ie:
                blk = jnp.where(lane < GRID_W, blk, tz_ref[0, io])
            bias_ref[qi * GRID_W:(qi + 1) * GRID_W, ap * 2 * GRID_W:(ap + 1) * 2 * GRID_W] = blk


def _natten_kernel(q_ref, k_ref, v_ref, qg_ref, kg_ref, tz_ref, o_ref, kn_ref, bias_ref):
    rb = pl.program_id(1)
    n_steps = pl.num_programs(1)
    s = k_ref.shape[0]
    rows = s // GRID_W
    rq, rk = NA_ROWS_PER_STEP, NA_WIN_ROWS
    kblk = 512

    @pl.when(rb == 0)
    def _():
        def body(i, carry):
            sl = pl.ds(pl.multiple_of(i * kblk, kblk), kblk)
            kk = k_ref[sl, :].astype(F32)
            ms = jnp.mean(kk * kk, axis=-1, keepdims=True)
            kn_ref[sl, :] = (kk * lax.rsqrt(ms + EPS) * kg_ref[...]).astype(BF16)
            return carry
        lax.fori_loop(0, s // kblk, body, 0)
        _natten_fill_bias(tz_ref, bias_ref, 0, 0, rows)

    @pl.when(rb == 1)
    def _():
        _natten_fill_bias(tz_ref, bias_ref, max(rq - NA_KH // 2, 0), rq, rows)

    @pl.when(rb == n_steps - 1)
    def _():
        _natten_fill_bias(tz_ref, bias_ref, rows - rk, rows - rq, rows)

    q = q_ref[...].astype(F32)
    ms = jnp.mean(q * q, axis=-1, keepdims=True)
    qn = (q * lax.rsqrt(ms + EPS) * (qg_ref[...] * (NA_HD ** -0.5))).astype(BF16)

    w0 = jnp.clip(rq * rb - NA_KH // 2, 0, rows - rk) * GRID_W
    w0 = pl.multiple_of(w0, GRID_W)
    win = rk * GRID_W
    kw = kn_ref[pl.ds(w0, win), :]
    vw = v_ref[pl.ds(w0, win), :]
    sc = _dot_nt(qn, kw) + bias_ref[...]
    m = jnp.max(sc, axis=-1, keepdims=True)
    p = jnp.exp(sc - m)
    l = jnp.sum(p, axis=-1, keepdims=True)
    o = _dot(p.astype(BF16), vw) / l
    o_ref[...] = o.astype(o_ref.dtype)


def _natten_toeplitz(rpb):
    c = np.arange(GRID_W)
    c_start = np.clip(c - NA_KW // 2, 0, GRID_W - NA_KW)
    col_in = (c[None, :] >= c_start[:, None]) & (c[None, :] < c_start[:, None] + NA_KW)
    col_off = np.clip(c[None, :] - c[:, None] + (NA_KW - 1), 0, 2 * NA_KW - 2)
    sel = ((col_off[..., None] == np.arange(2 * NA_KW - 1)) & col_in[..., None]).astype(np.float32)
    tz = jnp.einsum('hij,ckj->hick', rpb.astype(F32), sel, precision=lax.Precision.HIGHEST)
    tz = jnp.where(col_in[None, None], tz, NEG)
    tz = jnp.concatenate([tz, jnp.full_like(tz[:, :1], NEG)], axis=1)
    return jnp.concatenate([tz, tz], axis=-1)


def _natten(u, qg, kg, tz, col0):
    s = u.shape[0]
    tq = NA_ROWS_PER_STEP * GRID_W
    n_steps = s // tq
    assert n_steps >= 3 and s // GRID_W >= NA_WIN_ROWS
    return pl.pallas_call(
        _natten_kernel,
        grid=(NA_HEADS, n_steps),
        in_specs=[
            pl.BlockSpec((tq, NA_HD), lambda h, rb: (rb, col0 + h)),
            pl.BlockSpec((s, NA_HD), lambda h, rb: (0, col0 + NA_HEADS + h)),
            pl.BlockSpec((s, NA_HD), lambda h, rb: (0, col0 + 2 * NA_HEADS + h)),
            pl.BlockSpec((1, NA_HD), lambda h, rb: (0, 0)),
            pl.BlockSpec((1, NA_HD), lambda h, rb: (0, 0)),
            pl.BlockSpec((1,) + tz.shape[1:], lambda h, rb: (h, 0, 0, 0)),
        ],
        out_specs=pl.BlockSpec((tq, NA_HD), lambda h, rb: (rb, h)),
        out_shape=jax.ShapeDtypeStruct((s, NA_W), BF16),
        scratch_shapes=[pltpu.VMEM((s, NA_HD), BF16),
                        pltpu.VMEM((tq, NA_WIN_ROWS * GRID_W), F32)],
        compiler_params=pltpu.CompilerParams(
            dimension_semantics=("parallel", "arbitrary"),
            vmem_limit_bytes=VMEM_LIMIT),
        name="natten",
    )(u, u, u, qg.reshape(1, NA_HD), kg.reshape(1, NA_HD), tz)


def _merge_kernel(of_ref, ob_ref, hg_ref, nb_ref, ng_ref, ma0_ref, ma1_ref, mb0_ref, mb1_ref,
                  x_ref, p_ref, on_ref, pn_ref, wa_ref, wb_ref, wo_ref, wg_ref, wp_ref, o_ref):
    oa = of_ref[...].astype(F32) + ob_ref[...].astype(F32)
    gate_a = _silu(hg_ref[...].astype(F32))
    parts = []
    for h in range(HG_HEADS):
        hs = slice(h * HG_DK, (h + 1) * HG_DK)
        oh = oa[:, hs]
        ms = jnp.mean(oh * oh, axis=-1, keepdims=True)
        parts.append((oh * lax.rsqrt(ms + EPS) * on_ref[...] * gate_a[:, hs]).astype(BF16))
    a_in = jnp.concatenate(parts, axis=1)
    ya = _dot(a_in, wa_ref[...])
    b_in = (nb_ref[...].astype(F32) * _silu(ng_ref[...].astype(F32))).astype(BF16)
    yb = _dot(b_in, wb_ref[...])
    m_a = jnp.concatenate([ma0_ref[...], ma1_ref[...]], axis=1).astype(F32)
    m_b = jnp.concatenate([mb0_ref[...], mb1_ref[...]], axis=1).astype(F32)
    y = _sigmoid(m_a) * ya + _sigmoid(m_b) * yb
    x1 = x_ref[...] + _dot(y.astype(BF16), wo_ref[...])
    ms = jnp.mean(x1 * x1, axis=-1, keepdims=True)
    hn = (x1 * lax.rsqrt(ms + EPS) * pn_ref[...]).astype(BF16)
    g = _sigmoid(_dot(hn, wg_ref[...]))
    o_ref[...] = x1 + _dot(p_ref[...].astype(BF16), wp_ref[...]) * g


def _merge(o_f, o_b, u, nb, x, p, onorm, pnorm, wa, wb, wo, wg, wp, cb_hgate, cb_ngate, cb_ma, tm=256):
    s, d = x.shape
    tm = min(tm, s)
    pd = p.shape[1]
    act = lambda wdt: pl.BlockSpec((tm, wdt), lambda i: (i, 0))
    ucol = lambda cb: pl.BlockSpec((tm, HG_W), lambda i, cb=cb: (i, cb))
    const = lambda shp: pl.BlockSpec(shp, lambda i: (0, 0), pipeline_mode=pl.Buffered(1))
    return pl.pallas_call(
        _merge_kernel,
        grid=(s // tm,),
        in_specs=[act(HG_W), act(HG_W), ucol(cb_hgate), act(NA_W), ucol(cb_ngate),
                  ucol(cb_ma), ucol(cb_ma + 1), ucol(cb_ma + 2), ucol(cb_ma + 3),
                  act(d), act(pd),
                  const((1, HG_DK)), const((1, d)),
                  const(wa.shape), const(wb.shape), const(wo.shape), const(wg.shape), const(wp.shape)],
        out_specs=act(d),
        out_shape=jax.ShapeDtypeStruct((s, d), F32),
        compiler_params=pltpu.CompilerParams(
            dimension_semantics=("parallel",),
            vmem_limit_bytes=VMEM_LIMIT),
        name="merge",
    )(o_f, o_b, u, nb, u, u, u, u, u, x, p, onorm.reshape(1, HG_DK), pnorm.reshape(1, d),
      wa, wb, wo, wg, wp)


def kernel(x, p, norm_g, w_in, hgrn_lb, hgrn_onorm, na_qnorm, na_knorm, na_rpb,
           w_branch_a, w_branch_b, w_out, ple_norm, w_ple_gate, w_ple):
    bsz, s, d = x.shape
    depth = w_in.shape[0]
    rows = s // GRID_W
    lbp = jax.nn.softmax(hgrn_lb.astype(F32), axis=0)
    lower = jnp.cumsum(lbp, axis=0) - lbp[0:1]

    outs = []
    for bi in range(bsz):
        xb = x[bi]
        for i in range(depth):
            u = _in_proj(xb, norm_g[i], w_in[i].astype(BF16))
            o_f, o_b = _hgrn(u, lower[i])
            nb = _natten(u, na_qnorm[i], na_knorm[i], _natten_toeplitz(na_rpb[i]),
                         col0=5 * HG_W // NA_HD)
            xb = _merge(o_f, o_b, u, nb, xb, p[i, bi], hgrn_onorm[i], ple_norm[i],
                        w_branch_a[i].astype(BF16), w_branch_b[i].astype(BF16),
                        w_out[i].astype(BF16), w_ple_gate[i].astype(BF16), w_ple[i].astype(BF16),
                        cb_hgate=4, cb_ngate=8, cb_ma=9)
        outs.append(xb)
    return jnp.stack(outs, axis=0)
```

```python
import functools

import jax
import jax.numpy as jnp
import numpy as np
from jax import lax
from jax.experimental import pallas as pl
from jax.experimental.pallas import tpu as pltpu

F32 = jnp.float32
BF16 = jnp.bfloat16

GRID_W = 64
HG_HEADS = 8
HG_DK = 128
HG_W = HG_HEADS * HG_DK
NA_HEADS = 8
NA_HD = 128
NA_W = NA_HEADS * NA_HD
NA_KH = 8
NA_KW = 16
EPS = 1e-6
NEG = -1e30
F_MIN = 1e-6

CHUNK = 64
SUB = 8
NSUB = CHUNK // SUB
MID = SUB // 2

NA_ROWS_PER_STEP = 4
NA_WIN_ROWS = 12
NA_CHAINS = 4
LOG2E = 1.4426950408889634

VMEM_LIMIT = 56 * 1024 * 1024


def _dot(a, b):
    return jnp.dot(a, b, preferred_element_type=F32)


def _dot_nt(a, b):
    return lax.dot_general(a, b, (((1,), (1,)), ((), ())), preferred_element_type=F32)


def _dot_tn(a, b):
    return lax.dot_general(a, b, (((0,), (0,)), ((), ())), preferred_element_type=F32)


def _sigmoid(x):
    return 1.0 / (1.0 + jnp.exp(-x))


def _silu(x):
    return x * _sigmoid(x)


def _rms_bf16(x, g):
    ms = jnp.mean(x * x, axis=-1, keepdims=True)
    return (x * lax.rsqrt(ms + EPS) * g).astype(BF16)


def _rmsnorm_kernel(x_ref, g_ref, o_ref):
    o_ref[...] = _rms_bf16(x_ref[...], g_ref[...])


def _rmsnorm(x, g, tm=512):
    s, d = x.shape
    tm = min(tm, s)
    return pl.pallas_call(
        _rmsnorm_kernel,
        grid=(s // tm,),
        in_specs=[pl.BlockSpec((tm, d), lambda i: (i, 0)),
                  pl.BlockSpec((1, d), lambda i: (0, 0))],
        out_specs=pl.BlockSpec((tm, d), lambda i: (i, 0)),
        out_shape=jax.ShapeDtypeStruct((s, d), BF16),
        compiler_params=pltpu.CompilerParams(
            dimension_semantics=("parallel",),
            vmem_limit_bytes=VMEM_LIMIT),
        name="rmsnorm",
    )(x, g.reshape(1, d))


def _inproj_kernel(h_ref, w_ref, o_ref, wb_ref):
    @pl.when(pl.program_id(1) == 0)
    def _():
        wb_ref[...] = w_ref[...].astype(BF16)

    o_ref[...] = _dot(h_ref[...], wb_ref[...]).astype(o_ref.dtype)


def _in_proj(h, w, layer, tm=1024, tn=1024):
    s, d = h.shape
    n = w.shape[2]
    tm = min(tm, s)
    return pl.pallas_call(
        _inproj_kernel,
        grid=(n // tn, s // tm),
        in_specs=[
            pl.BlockSpec((tm, d), lambda j, i: (i, 0)),
            pl.BlockSpec((None, d, tn), lambda j, i: (layer, 0, j)),
        ],
        out_specs=pl.BlockSpec((tm, tn), lambda j, i: (i, j)),
        out_shape=jax.ShapeDtypeStruct((s, n), BF16),
        scratch_shapes=[pltpu.VMEM((d, tn), BF16)],
        compiler_params=pltpu.CompilerParams(
            dimension_semantics=("parallel", "arbitrary"),
            vmem_limit_bytes=VMEM_LIMIT),
        name="in_proj",
    )(h, w)


def _hgrn_selectors():
    c = CHUNK
    t = np.arange(c)[:, None]
    r = np.arange(c)[None, :]
    m = (t // SUB) * SUB + MID
    out = []
    for backward in (False, True):
        if backward:
            cum = lambda p: (r >= p).astype(np.float32)
            prev_m = m - SUB
        else:
            cum = lambda p: (r <= p).astype(np.float32)
            prev_m = m + SUB
        has_prev = (prev_m >= 0) & (prev_m < c)
        step = np.where(has_prev, cum(np.clip(prev_m, 0, c - 1)) - cum(m), 0.0)
        total = np.ones((16, c), np.float32)
        sel = np.concatenate([cum(t) - cum(m), cum(m), 1.0 - cum(m), step, total], axis=0)
        out.append(np.concatenate([sel, sel], axis=1))
    return np.stack(out)


HG_SEL_ROWS = 4 * CHUNK + 16


def _hgrn_prologue(q_ref, v_ref, z_ref, lb, sel, backward):
    c = CHUNK
    w = q_ref.shape[1]
    q = q_ref[...].astype(F32)
    v = v_ref[...]
    z = z_ref[...].astype(F32)

    f = lb + (1.0 - lb) * _sigmoid(z)
    g = jnp.log2(jnp.maximum(f, F_MIN))
    k = 1.0 - f
    qs = q * _sigmoid(q)

    hi = g.astype(BF16)
    lo = (g - hi.astype(F32)).astype(BF16)
    cums = _dot(sel, jnp.concatenate([hi, lo], axis=0))
    e = jnp.exp2(cums[0:c])
    to_ref = jnp.exp2(cums[c:2 * c])
    from_ref = jnp.exp2(cums[2 * c:3 * c])
    step = jnp.exp2(cums[3 * c:4 * c])
    dec = jnp.exp2(cums[4 * c:4 * c + SUB])

    q0 = qs * e
    k0 = k * (1.0 / e)
    q_hat = (q0 * to_ref).astype(BF16)
    k_hat = (k0 * from_ref).astype(BF16)

    blocks = [slice(j * SUB, (j + 1) * SUB) for j in range(NSUB)]
    zero_blk = jnp.zeros((SUB, w), F32)
    order = range(NSUB) if backward else range(NSUB - 1, -1, -1)
    cur = [None] * NSUB
    q_tilde = [None] * NSUB
    k_tilde = [None] * NSUB
    for j in order:
        for jj in range(NSUB):
            if cur[jj] is not None:
                cur[jj] = cur[jj] * step[blocks[j]]
        cur[j] = q0[blocks[j]]
        q_tilde[j] = jnp.concatenate(
            [zero_blk if cur[jj] is None else cur[jj] for jj in range(NSUB)], axis=0).astype(BF16)
        k_tilde[j] = jnp.concatenate(
            [k0[blocks[jj]] if jj == j else zero_blk for jj in range(NSUB)], axis=0).astype(BF16)

    row = lax.broadcasted_iota(jnp.int32, (c, c), 0)
    col = lax.broadcasted_iota(jnp.int32, (c, c), 1)
    causal = (row <= col) if backward else (row >= col)
    heads = []
    for h in range(w // HG_DK):
        hs = slice(h * HG_DK, (h + 1) * HG_DK)
        heads.append(dict(
            q_hat=q_hat[:, hs], k_hat=k_hat[:, hs], v=v[:, hs], dec=dec[:, hs],
            q_cat=jnp.concatenate([q_tilde[j][:, hs] for j in range(NSUB)], axis=1),
            k_cat=jnp.concatenate([k_tilde[j][:, hs] for j in range(NSUB)], axis=1)))
    return heads, causal


def _hgrn_kernel(qf_ref, vf_ref, zf_ref, qb_ref, vb_ref, zb_ref, lb_ref, sel_ref,
                 of_ref, ob_ref, stf_ref, stb_ref):
    @pl.when(pl.program_id(0) == 0)
    def _():
        stf_ref[...] = jnp.zeros_like(stf_ref)
        stb_ref[...] = jnp.zeros_like(stb_ref)

    heads_f, causal_f = _hgrn_prologue(qf_ref, vf_ref, zf_ref, lb_ref[0:1, :], sel_ref[0], False)
    heads_b, causal_b = _hgrn_prologue(qb_ref, vb_ref, zb_ref, lb_ref[1:2, :], sel_ref[1], True)
    n_heads = len(heads_f)
    work = ([(heads_f[h], causal_f, of_ref, stf_ref, h) for h in range(n_heads)]
            + [(heads_b[h], causal_b, ob_ref, stb_ref, h) for h in range(n_heads)])

    states = [st_ref[h] for (_, _, _, st_ref, h) in work]
    atts = [_dot_nt(a["q_cat"], a["k_cat"]) for (a, _, _, _, _) in work]
    v_ts = [a["v"].T for (a, _, _, _, _) in work]
    updates = [_dot(v_t, a["k_hat"]) for v_t, (a, _, _, _, _) in zip(v_ts, work)]
    atts = [jnp.where(causal, att, 0.0).astype(BF16) for att, (_, causal, _, _, _) in zip(atts, work)]
    outs = [_dot_nt(jnp.concatenate([a["q_hat"], att], axis=1),
                    jnp.concatenate([st.astype(BF16), v_t], axis=1))
            for att, st, v_t, (a, _, _, _, _) in zip(atts, states, v_ts, work)]
    for (a, _, o_ref, st_ref, h), st, out, upd in zip(work, states, outs, updates):
        o_ref[:, h * HG_DK:(h + 1) * HG_DK] = out.astype(o_ref.dtype)
        decayed = st.reshape(HG_DK // SUB, SUB, HG_DK) * a["dec"][None]
        st_ref[h] = decayed.reshape(HG_DK, HG_DK) + upd


def _hgrn(u, lb):
    s = u.shape[0]
    n = s // CHUNK
    fwd = lambda cb: pl.BlockSpec((CHUNK, HG_W), lambda i, cb=cb: (i, cb))
    bwd = lambda cb: pl.BlockSpec((CHUNK, HG_W), lambda i, cb=cb: (n - 1 - i, cb))
    heads = HG_W // HG_DK
    return pl.pallas_call(
        _hgrn_kernel,
        grid=(n,),
        in_specs=[fwd(0), fwd(1), fwd(2), bwd(0), bwd(1), bwd(3),
                  pl.BlockSpec((2, HG_W), lambda i: (0, 0)),
                  pl.BlockSpec((2, HG_SEL_ROWS, 2 * CHUNK), lambda i: (0, 0, 0))],
        out_specs=[pl.BlockSpec((CHUNK, HG_W), lambda i: (i, 0)),
                   pl.BlockSpec((CHUNK, HG_W), lambda i: (n - 1 - i, 0))],
        out_shape=[jax.ShapeDtypeStruct((s, HG_W), BF16)] * 2,
        scratch_shapes=[pltpu.VMEM((heads, HG_DK, HG_DK), F32)] * 2,
        compiler_params=pltpu.CompilerParams(
            dimension_semantics=("arbitrary",),
            vmem_limit_bytes=VMEM_LIMIT),
        name="hgrn",
    )(u, u, u, u, u, u, lb, jnp.asarray(_hgrn_selectors(), BF16))


NA_NEG_BLOCK = 2 * NA_KH - 1


def _natten_fill_bias(tz_ref, bias_ref, w0, r0, rows):
    kh = min(NA_KH, rows)
    lane = lax.broadcasted_iota(jnp.int32, (GRID_W, 2 * GRID_W), 1)

    def block_index(qi, a):
        r = r0 + qi
        r_start = min(max(r - kh // 2, 0), rows - kh)
        ar = w0 + a
        if r_start <= ar < r_start + kh:
            return ar - r + (NA_KH - 1)
        return NA_NEG_BLOCK

    for qi in range(NA_ROWS_PER_STEP):
        for ap in range(NA_WIN_ROWS // 2):
            ie, io = block_index(qi, 2 * ap), block_index(qi, 2 * ap + 1)
            blk = tz_ref[0, ie]
            if io != ie:
                blk = jnp.where(lane < GRID_W, blk, tz_ref[0, io])
            bias_ref[qi * GRID_W:(qi + 1) * GRID_W, ap * 2 * GRID_W:(ap + 1) * 2 * GRID_W] = blk


def _natten_kernel(q_ref, k_ref, v_ref, qg_ref, kg_ref, tz_ref, o_ref, kn_ref, bias_ref):
    step = pl.program_id(1)
    s = k_ref.shape[0]
    rows = s // GRID_W
    rq, rk = NA_ROWS_PER_STEP, NA_WIN_ROWS
    n_blocks = rows // rq
    tq, win = rq * GRID_W, rk * GRID_W
    kblk = 512

    @pl.when(step == 0)
    def _():
        def body(i, carry):
            sl = pl.ds(pl.multiple_of(i * kblk, kblk), kblk)
            kk = k_ref[sl, :].astype(F32)
            ms = jnp.mean(kk * kk, axis=-1, keepdims=True)
            kn_ref[sl, :] = (kk * lax.rsqrt(ms + EPS) * kg_ref[...]).astype(BF16)
            return carry
        lax.fori_loop(0, s // kblk, body, 0)
        _natten_fill_bias(tz_ref, bias_ref.at[0], 0, 0, rows)
        _natten_fill_bias(tz_ref, bias_ref.at[1], max(rq - NA_KH // 2, 0), rq, rows)
        _natten_fill_bias(tz_ref, bias_ref.at[2], rows - rk, rows - rq, rows)

    q = q_ref[...].astype(F32)
    ms = jnp.mean(q * q, axis=-1, keepdims=True)
    qn = (q * lax.rsqrt(ms + EPS) * (qg_ref[...] * (NA_HD ** -0.5 * LOG2E))).astype(BF16)

    qs, kws, vws, variants = [], [], [], []
    for c in range(NA_CHAINS):
        rb = step * NA_CHAINS + c
        w0 = jnp.clip(rq * rb - NA_KH // 2, 0, rows - rk) * GRID_W
        w0 = pl.multiple_of(w0, GRID_W)
        qs.append(qn[c * tq:(c + 1) * tq])
        kws.append(kn_ref[pl.ds(w0, win), :])
        vws.append(v_ref[pl.ds(w0, win), :])
        variants.append(jnp.where(rb == 0, 0, jnp.where(rb == n_blocks - 1, 2, 1)))
    scs = [_dot_nt(qc, kw) for qc, kw in zip(qs, kws)]
    scs = [sc + bias_ref[var] for sc, var in zip(scs, variants)]
    ms = [jnp.max(sc, axis=-1, keepdims=True) for sc in scs]
    ps = [jnp.exp2(sc - m) for sc, m in zip(scs, ms)]
    ls = [jnp.sum(p, axis=-1, keepdims=True) for p in ps]
    outs = [_dot(p.astype(BF16), vw) for p, vw in zip(ps, vws)]
    for c, (o, l) in enumerate(zip(outs, ls)):
        o_ref[c * tq:(c + 1) * tq, :] = (o / l).astype(o_ref.dtype)


def _natten_toeplitz(rpb):
    c = np.arange(GRID_W)
    c_start = np.clip(c - NA_KW // 2, 0, GRID_W - NA_KW)
    col_in = (c[None, :] >= c_start[:, None]) & (c[None, :] < c_start[:, None] + NA_KW)
    col_off = np.clip(c[None, :] - c[:, None] + (NA_KW - 1), 0, 2 * NA_KW - 2)
    sel = ((col_off[..., None] == np.arange(2 * NA_KW - 1)) & col_in[..., None]).astype(np.float32)
    tz = jnp.einsum('hij,ckj->hick', rpb.astype(F32), sel, precision=lax.Precision.HIGHEST)
    tz = jnp.where(col_in[None, None], tz * LOG2E, NEG)
    tz = jnp.concatenate([tz, jnp.full_like(tz[:, :1], NEG)], axis=1)
    return jnp.concatenate([tz, tz], axis=-1)


def _natten(u, qg, kg, tz, col0):
    s = u.shape[0]
    blk = NA_ROWS_PER_STEP * GRID_W
    tq = NA_CHAINS * blk
    n_steps = s // tq
    assert s % tq == 0 and s // blk >= 3 and s // GRID_W >= NA_WIN_ROWS
    return pl.pallas_call(
        _natten_kernel,
        grid=(NA_HEADS, n_steps),
        in_specs=[
            pl.BlockSpec((tq, NA_HD), lambda h, rb: (rb, col0 + h)),
            pl.BlockSpec((s, NA_HD), lambda h, rb: (0, col0 + NA_HEADS + h)),
            pl.BlockSpec((s, NA_HD), lambda h, rb: (0, col0 + 2 * NA_HEADS + h)),
            pl.BlockSpec((1, NA_HD), lambda h, rb: (0, 0)),
            pl.BlockSpec((1, NA_HD), lambda h, rb: (0, 0)),
            pl.BlockSpec((1,) + tz.shape[1:], lambda h, rb: (h, 0, 0, 0)),
        ],
        out_specs=pl.BlockSpec((tq, NA_HD), lambda h, rb: (rb, h)),
        out_shape=jax.ShapeDtypeStruct((s, NA_W), BF16),
        scratch_shapes=[pltpu.VMEM((s, NA_HD), BF16),
                        pltpu.VMEM((3, blk, NA_WIN_ROWS * GRID_W), F32)],
        compiler_params=pltpu.CompilerParams(
            dimension_semantics=("parallel", "arbitrary"),
            vmem_limit_bytes=VMEM_LIMIT),
        name="natten",
    )(u, u, u, qg.reshape(1, NA_HD), kg.reshape(1, NA_HD), tz)


def _merge_kernel(of_ref, ob_ref, hg_ref, nb_ref, ng_ref, ma0_ref, ma1_ref, mb0_ref, mb1_ref,
                  x_ref, p_ref, on_ref, pn_ref, gn_ref, wa_ref, wb_ref, wo_ref, wg_ref, wp_ref,
                  o_ref, *maybe_hn_ref):
    oa = of_ref[...].astype(F32) + ob_ref[...].astype(F32)
    gate_a = _silu(hg_ref[...].astype(F32))
    parts = []
    for h in range(HG_HEADS):
        hs = slice(h * HG_DK, (h + 1) * HG_DK)
        oh = oa[:, hs]
        ms = jnp.mean(oh * oh, axis=-1, keepdims=True)
        parts.append((oh * lax.rsqrt(ms + EPS) * on_ref[...] * gate_a[:, hs]).astype(BF16))
    a_in = jnp.concatenate(parts, axis=1)
    ya = _dot(a_in, wa_ref[...])
    b_in = (nb_ref[...].astype(F32) * _silu(ng_ref[...].astype(F32))).astype(BF16)
    yb = _dot(b_in, wb_ref[...])
    m_a = jnp.concatenate([ma0_ref[...], ma1_ref[...]], axis=1).astype(F32)
    m_b = jnp.concatenate([mb0_ref[...], mb1_ref[...]], axis=1).astype(F32)
    y = _sigmoid(m_a) * ya + _sigmoid(m_b) * yb
    x1 = x_ref[...] + _dot(y.astype(BF16), wo_ref[...])
    g = _sigmoid(_dot(_rms_bf16(x1, pn_ref[...]), wg_ref[...]))
    x2 = x1 + _dot(p_ref[...].astype(BF16), wp_ref[...]) * g
    o_ref[...] = x2
    for hn_ref in maybe_hn_ref:
        hn_ref[...] = _rms_bf16(x2, gn_ref[...])


def _merge(o_f, o_b, u, nb, x, p, p_index, onorm, pnorm, next_norm, weights, layer,
           cb_hgate, cb_ngate, cb_ma, tm=256):
    s, d = x.shape
    tm = min(tm, s)
    pd = p.shape[-1]
    act = lambda wdt: pl.BlockSpec((tm, wdt), lambda i: (i, 0))
    ucol = lambda cb: pl.BlockSpec((tm, HG_W), lambda i, cb=cb: (i, cb))
    vec = lambda n: pl.BlockSpec((1, n), lambda i: (0, 0), pipeline_mode=pl.Buffered(1))
    wspec = lambda w: pl.BlockSpec((None,) + w.shape[1:], lambda i: (layer, 0, 0),
                                   pipeline_mode=pl.Buffered(1))
    emit_next = next_norm is not None
    gn = next_norm if emit_next else pnorm
    out_specs = [act(d)] + ([act(d)] if emit_next else [])
    out_shape = [jax.ShapeDtypeStruct((s, d), F32)] + (
        [jax.ShapeDtypeStruct((s, d), BF16)] if emit_next else [])
    res = pl.pallas_call(
        _merge_kernel,
        grid=(s // tm,),
        in_specs=[act(HG_W), act(HG_W), ucol(cb_hgate), act(NA_W), ucol(cb_ngate),
                  ucol(cb_ma), ucol(cb_ma + 1), ucol(cb_ma + 2), ucol(cb_ma + 3),
                  act(d), pl.BlockSpec((None, tm, pd), lambda i: (p_index, i, 0)),
                  vec(HG_DK), vec(d), vec(d)] + [wspec(w) for w in weights],
        out_specs=out_specs,
        out_shape=out_shape,
        compiler_params=pltpu.CompilerParams(
            dimension_semantics=("parallel",),
            vmem_limit_bytes=VMEM_LIMIT),
        name="merge",
    )(o_f, o_b, u, nb, u, u, u, u, u, x, p, onorm.reshape(1, HG_DK), pnorm.reshape(1, d),
      gn.reshape(1, d), *weights)
    return (res[0], res[1]) if emit_next else (res[0], None)


def kernel(x, p, norm_g, w_in, hgrn_lb, hgrn_onorm, na_qnorm, na_knorm, na_rpb,
           w_branch_a, w_branch_b, w_out, ple_norm, w_ple_gate, w_ple):
    bsz, s, d = x.shape
    depth = w_in.shape[0]
    lbp = jax.nn.softmax(hgrn_lb.astype(F32), axis=0)
    lower = jnp.cumsum(lbp, axis=0) - lbp[0:1]
    weights = [w.astype(BF16) for w in (w_branch_a, w_branch_b, w_out, w_ple_gate, w_ple)]
    p_flat = p.reshape(depth * bsz, s, p.shape[-1])

    outs = []
    for bi in range(bsz):
        xb = x[bi]
        h = _rmsnorm(xb, norm_g[0])
        for i in range(depth):
            u = _in_proj(h, w_in, i)
            o_f, o_b = _hgrn(u, lower[i])
            nb = _natten(u, na_qnorm[i], na_knorm[i], _natten_toeplitz(na_rpb[i]),
                         col0=5 * HG_W // NA_HD)
            xb, h = _merge(o_f, o_b, u, nb, xb, p_flat, i * bsz + bi, hgrn_onorm[i], ple_norm[i],
                           norm_g[i + 1] if i + 1 < depth else None, weights, i,
                           cb_hgate=4, cb_ngate=8, cb_ma=9)
        outs.append(xb)
    return jnp.stack(outs, axis=0)
```

```python
import functools

import jax
import jax.numpy as jnp
import numpy as np
from jax import lax
from jax.experimental import pallas as pl
from jax.experimental.pallas import tpu as pltpu

F32 = jnp.float32
BF16 = jnp.bfloat16

GRID_W = 64
HG_HEADS = 8
HG_DK = 128
HG_W = HG_HEADS * HG_DK
NA_HEADS = 8
NA_HD = 128
NA_W = NA_HEADS * NA_HD
NA_KH = 8
NA_KW = 16
EPS = 1e-6
NEG = -1e30
F_MIN = 1e-6

CHUNK = 64
HG_CHUNKS_PER_STEP = 2
SUB = 8
NSUB = CHUNK // SUB
MID = SUB // 2

NA_ROWS_PER_STEP = 4
NA_WIN_ROWS = 12
NA_CHAINS = 4
MERGE_SUB_ROWS = 128
LOG2E = 1.4426950408889634

VMEM_LIMIT = 56 * 1024 * 1024


def _dot(a, b):
    return jnp.dot(a, b, preferred_element_type=F32)


def _dot_nt(a, b):
    return lax.dot_general(a, b, (((1,), (1,)), ((), ())), preferred_element_type=F32)


def _dot_tn(a, b):
    return lax.dot_general(a, b, (((0,), (0,)), ((), ())), preferred_element_type=F32)


def _sigmoid(x):
    return 1.0 / (1.0 + jnp.exp(-x))


def _silu(x):
    return x * _sigmoid(x)


def _rms_bf16(x, g):
    ms = jnp.mean(x * x, axis=-1, keepdims=True)
    return (x * lax.rsqrt(ms + EPS) * g).astype(BF16)


def _rmsnorm_kernel(x_ref, g_ref, o_ref):
    o_ref[...] = _rms_bf16(x_ref[...], g_ref[...])


def _rmsnorm(x, g, tm=512):
    s, d = x.shape
    tm = min(tm, s)
    return pl.pallas_call(
        _rmsnorm_kernel,
        grid=(s // tm,),
        in_specs=[pl.BlockSpec((tm, d), lambda i: (i, 0)),
                  pl.BlockSpec((1, d), lambda i: (0, 0))],
        out_specs=pl.BlockSpec((tm, d), lambda i: (i, 0)),
        out_shape=jax.ShapeDtypeStruct((s, d), BF16),
        compiler_params=pltpu.CompilerParams(
            dimension_semantics=("parallel",),
            vmem_limit_bytes=VMEM_LIMIT),
        name="rmsnorm",
    )(x, g.reshape(1, d))


def _inproj_kernel(h_ref, w_ref, o_ref, wb_ref):
    @pl.when(pl.program_id(1) == 0)
    def _():
        wb_ref[...] = w_ref[...].astype(BF16)

    o_ref[...] = _dot(h_ref[...], wb_ref[...]).astype(o_ref.dtype)


def _in_proj(h, w, layer, tm=2048, tn=1024):
    s, d = h.shape
    n = w.shape[2]
    tm = min(tm, s)
    return pl.pallas_call(
        _inproj_kernel,
        grid=(n // tn, s // tm),
        in_specs=[
            pl.BlockSpec((tm, d), lambda j, i: (i, 0)),
            pl.BlockSpec((None, d, tn), lambda j, i: (layer, 0, j)),
        ],
        out_specs=pl.BlockSpec((tm, tn), lambda j, i: (i, j)),
        out_shape=jax.ShapeDtypeStruct((s, n), BF16),
        scratch_shapes=[pltpu.VMEM((d, tn), BF16)],
        compiler_params=pltpu.CompilerParams(
            dimension_semantics=("parallel", "arbitrary"),
            vmem_limit_bytes=VMEM_LIMIT),
        name="in_proj",
    )(h, w)


def _hgrn_selectors():
    c = CHUNK
    t = np.arange(c)[:, None]
    r = np.arange(c)[None, :]
    m = (t // SUB) * SUB + MID
    out = []
    for backward in (False, True):
        if backward:
            cum = lambda p: (r >= p).astype(np.float32)
            prev_m = m - SUB
        else:
            cum = lambda p: (r <= p).astype(np.float32)
            prev_m = m + SUB
        has_prev = (prev_m >= 0) & (prev_m < c)
        step = np.where(has_prev, cum(np.clip(prev_m, 0, c - 1)) - cum(m), 0.0)
        total = np.ones((16, c), np.float32)
        sel = np.concatenate([cum(t) - cum(m), cum(m), 1.0 - cum(m), step, total], axis=0)
        out.append(np.concatenate([sel, sel], axis=1))
    return np.stack(out)


HG_SEL_ROWS = 4 * CHUNK + 16


def _hgrn_prologue(q_ref, v_ref, z_ref, rows, lb, sel, backward):
    c = CHUNK
    w = q_ref.shape[1]
    q = q_ref[rows, :].astype(F32)
    v = v_ref[rows, :]
    z = z_ref[rows, :].astype(F32)

    f = lb + (1.0 - lb) * _sigmoid(z)
    g = jnp.log2(jnp.maximum(f, F_MIN))
    k = 1.0 - f
    qs = q * _sigmoid(q)

    hi = g.astype(BF16)
    lo = (g - hi.astype(F32)).astype(BF16)
    cums = _dot(sel, jnp.concatenate([hi, lo], axis=0))
    e = jnp.exp2(cums[0:c])
    to_ref = jnp.exp2(cums[c:2 * c])
    from_ref = jnp.exp2(cums[2 * c:3 * c])
    step = jnp.exp2(cums[3 * c:4 * c])
    dec = jnp.exp2(cums[4 * c:4 * c + SUB])

    q0 = qs * e
    k0 = k * (1.0 / e)
    q_hat = (q0 * to_ref).astype(BF16)
    k_hat = (k0 * from_ref).astype(BF16)

    blocks = [slice(j * SUB, (j + 1) * SUB) for j in range(NSUB)]
    zero_blk = jnp.zeros((SUB, w), F32)
    order = range(NSUB) if backward else range(NSUB - 1, -1, -1)
    cur = [None] * NSUB
    q_tilde = [None] * NSUB
    k_tilde = [None] * NSUB
    for j in order:
        for jj in range(NSUB):
            if cur[jj] is not None:
                cur[jj] = cur[jj] * step[blocks[j]]
        cur[j] = q0[blocks[j]]
        q_tilde[j] = jnp.concatenate(
            [zero_blk if cur[jj] is None else cur[jj] for jj in range(NSUB)], axis=0).astype(BF16)
        k_tilde[j] = jnp.concatenate(
            [k0[blocks[jj]] if jj == j else zero_blk for jj in range(NSUB)], axis=0).astype(BF16)

    row = lax.broadcasted_iota(jnp.int32, (c, c), 0)
    col = lax.broadcasted_iota(jnp.int32, (c, c), 1)
    causal = (row <= col) if backward else (row >= col)
    heads = []
    for h in range(w // HG_DK):
        hs = slice(h * HG_DK, (h + 1) * HG_DK)
        heads.append(dict(
            q_hat=q_hat[:, hs], k_hat=k_hat[:, hs], v=v[:, hs], dec=dec[:, hs],
            q_cat=jnp.concatenate([q_tilde[j][:, hs] for j in range(NSUB)], axis=1),
            k_cat=jnp.concatenate([k_tilde[j][:, hs] for j in range(NSUB)], axis=1)))
    return heads, causal


def _hgrn_kernel(qf_ref, vf_ref, zf_ref, qb_ref, vb_ref, zb_ref, lb_ref, sel_ref,
                 of_ref, ob_ref, stf_ref, stb_ref):
    @pl.when(pl.program_id(0) == 0)
    def _():
        stf_ref[...] = jnp.zeros_like(stf_ref)
        stb_ref[...] = jnp.zeros_like(stb_ref)

    n_heads = stf_ref.shape[0]
    n_chunks = qf_ref.shape[0] // CHUNK
    rows_f = [slice(c * CHUNK, (c + 1) * CHUNK) for c in range(n_chunks)]
    rows_b = rows_f[::-1]
    pro = [(_hgrn_prologue(qf_ref, vf_ref, zf_ref, rf, lb_ref[0:1, :], sel_ref[0], False),
            _hgrn_prologue(qb_ref, vb_ref, zb_ref, rb, lb_ref[1:2, :], sel_ref[1], True))
           for rf, rb in zip(rows_f, rows_b)]

    states = [stf_ref[h] for h in range(n_heads)] + [stb_ref[h] for h in range(n_heads)]
    for (rf, rb), ((heads_f, causal_f), (heads_b, causal_b)) in zip(zip(rows_f, rows_b), pro):
        work = ([(heads_f[h], causal_f, of_ref, rf, h) for h in range(n_heads)]
                + [(heads_b[h], causal_b, ob_ref, rb, h) for h in range(n_heads)])
        atts = [_dot_nt(a["q_cat"], a["k_cat"]) for (a, _, _, _, _) in work]
        v_ts = [a["v"].T for (a, _, _, _, _) in work]
        updates = [_dot(v_t, a["k_hat"]) for v_t, (a, _, _, _, _) in zip(v_ts, work)]
        atts = [jnp.where(causal, att, 0.0).astype(BF16)
                for att, (_, causal, _, _, _) in zip(atts, work)]
        outs = [_dot_nt(jnp.concatenate([a["q_hat"], att], axis=1),
                        jnp.concatenate([st.astype(BF16), v_t], axis=1))
                for att, st, v_t, (a, _, _, _, _) in zip(atts, states, v_ts, work)]
        new_states = []
        for (a, _, o_ref, rows, h), st, out, upd in zip(work, states, outs, updates):
            o_ref[rows, h * HG_DK:(h + 1) * HG_DK] = out.astype(o_ref.dtype)
            decayed = st.reshape(HG_DK // SUB, SUB, HG_DK) * a["dec"][None]
            new_states.append(decayed.reshape(HG_DK, HG_DK) + upd)
        states = new_states
    for h in range(n_heads):
        stf_ref[h] = states[h]
        stb_ref[h] = states[n_heads + h]


def _hgrn(u, lb):
    s = u.shape[0]
    t = min(HG_CHUNKS_PER_STEP * CHUNK, s)
    n = s // t
    fwd = lambda cb: pl.BlockSpec((t, HG_W), lambda i, cb=cb: (i, cb))
    bwd = lambda cb: pl.BlockSpec((t, HG_W), lambda i, cb=cb: (n - 1 - i, cb))
    heads = HG_W // HG_DK
    return pl.pallas_call(
        _hgrn_kernel,
        grid=(n,),
        in_specs=[fwd(0), fwd(1), fwd(2), bwd(0), bwd(1), bwd(3),
                  pl.BlockSpec((2, HG_W), lambda i: (0, 0)),
                  pl.BlockSpec((2, HG_SEL_ROWS, 2 * CHUNK), lambda i: (0, 0, 0))],
        out_specs=[pl.BlockSpec((t, HG_W), lambda i: (i, 0)),
                   pl.BlockSpec((t, HG_W), lambda i: (n - 1 - i, 0))],
        out_shape=[jax.ShapeDtypeStruct((s, HG_W), BF16)] * 2,
        scratch_shapes=[pltpu.VMEM((heads, HG_DK, HG_DK), F32)] * 2,
        compiler_params=pltpu.CompilerParams(
            dimension_semantics=("arbitrary",),
            vmem_limit_bytes=VMEM_LIMIT),
        name="hgrn",
    )(u, u, u, u, u, u, lb, jnp.asarray(_hgrn_selectors(), BF16))


NA_NEG_BLOCK = 2 * NA_KH - 1


def _natten_fill_bias(tz_ref, bias_ref, w0, r0, rows):
    kh = min(NA_KH, rows)
    lane = lax.broadcasted_iota(jnp.int32, (GRID_W, 2 * GRID_W), 1)

    def block_index(qi, a):
        r = r0 + qi
        r_start = min(max(r - kh // 2, 0), rows - kh)
        ar = w0 + a
        if r_start <= ar < r_start + kh:
            return ar - r + (NA_KH - 1)
        return NA_NEG_BLOCK

    for qi in range(NA_ROWS_PER_STEP):
        for ap in range(NA_WIN_ROWS // 2):
            ie, io = block_index(qi, 2 * ap), block_index(qi, 2 * ap + 1)
            blk = tz_ref[0, ie]
            if io != ie:
                blk = jnp.where(lane < GRID_W, blk, tz_ref[0, io])
            bias_ref[qi * GRID_W:(qi + 1) * GRID_W, ap * 2 * GRID_W:(ap + 1) * 2 * GRID_W] = blk


def _natten_kernel(q_ref, k_ref, v_ref, qg_ref, kg_ref, tz_ref, o_ref, kn_ref, bias_ref):
    step = pl.program_id(1)
    s = k_ref.shape[0]
    rows = s // GRID_W
    rq, rk = NA_ROWS_PER_STEP, NA_WIN_ROWS
    n_blocks = rows // rq
    tq, win = rq * GRID_W, rk * GRID_W
    kblk = 512

    @pl.when(step == 0)
    def _():
        def body(i, carry):
            sl = pl.ds(pl.multiple_of(i * kblk, kblk), kblk)
            kk = k_ref[sl, :].astype(F32)
            ms = jnp.mean(kk * kk, axis=-1, keepdims=True)
            kn_ref[sl, :] = (kk * lax.rsqrt(ms + EPS) * kg_ref[...]).astype(BF16)
            return carry
        lax.fori_loop(0, s // kblk, body, 0)
        _natten_fill_bias(tz_ref, bias_ref.at[0], 0, 0, rows)
        _natten_fill_bias(tz_ref, bias_ref.at[1], max(rq - NA_KH // 2, 0), rq, rows)
        _natten_fill_bias(tz_ref, bias_ref.at[2], rows - rk, rows - rq, rows)

    q = q_ref[...].astype(F32)
    ms = jnp.mean(q * q, axis=-1, keepdims=True)
    qn = (q * lax.rsqrt(ms + EPS) * (qg_ref[...] * (NA_HD ** -0.5 * LOG2E))).astype(BF16)

    qs, kws, vws, variants = [], [], [], []
    for c in range(NA_CHAINS):
        rb = step * NA_CHAINS + c
        w0 = jnp.clip(rq * rb - NA_KH // 2, 0, rows - rk) * GRID_W
        w0 = pl.multiple_of(w0, GRID_W)
        qs.append(qn[c * tq:(c + 1) * tq])
        kws.append(kn_ref[pl.ds(w0, win), :])
        vws.append(v_ref[pl.ds(w0, win), :])
        variants.append(jnp.where(rb == 0, 0, jnp.where(rb == n_blocks - 1, 2, 1)))
    scs = [_dot_nt(qc, kw) for qc, kw in zip(qs, kws)]
    scs = [sc + bias_ref[var] for sc, var in zip(scs, variants)]
    ms = [jnp.max(sc, axis=-1, keepdims=True) for sc in scs]
    ps = [jnp.exp2(sc - m) for sc, m in zip(scs, ms)]
    ls = [jnp.sum(p, axis=-1, keepdims=True) for p in ps]
    outs = [_dot(p.astype(BF16), vw) for p, vw in zip(ps, vws)]
    for c, (o, l) in enumerate(zip(outs, ls)):
        o_ref[c * tq:(c + 1) * tq, :] = (o / l).astype(o_ref.dtype)


def _natten_toeplitz(rpb):
    c = np.arange(GRID_W)
    c_start = np.clip(c - NA_KW // 2, 0, GRID_W - NA_KW)
    col_in = (c[None, :] >= c_start[:, None]) & (c[None, :] < c_start[:, None] + NA_KW)
    col_off = np.clip(c[None, :] - c[:, None] + (NA_KW - 1), 0, 2 * NA_KW - 2)
    sel = ((col_off[..., None] == np.arange(2 * NA_KW - 1)) & col_in[..., None]).astype(np.float32)
    tz = jnp.einsum('hij,ckj->hick', rpb.astype(F32), sel, precision=lax.Precision.HIGHEST)
    tz = jnp.where(col_in[None, None], tz * LOG2E, NEG)
    tz = jnp.concatenate([tz, jnp.full_like(tz[:, :1], NEG)], axis=1)
    return jnp.concatenate([tz, tz], axis=-1)


def _natten(u, qg, kg, tz, col0):
    s = u.shape[0]
    blk = NA_ROWS_PER_STEP * GRID_W
    tq = NA_CHAINS * blk
    n_steps = s // tq
    assert s % tq == 0 and s // blk >= 3 and s // GRID_W >= NA_WIN_ROWS
    return pl.pallas_call(
        _natten_kernel,
        grid=(NA_HEADS, n_steps),
        in_specs=[
            pl.BlockSpec((tq, NA_HD), lambda h, rb: (rb, col0 + h)),
            pl.BlockSpec((s, NA_HD), lambda h, rb: (0, col0 + NA_HEADS + h)),
            pl.BlockSpec((s, NA_HD), lambda h, rb: (0, col0 + 2 * NA_HEADS + h)),
            pl.BlockSpec((1, NA_HD), lambda h, rb: (0, 0)),
            pl.BlockSpec((1, NA_HD), lambda h, rb: (0, 0)),
            pl.BlockSpec((1,) + tz.shape[1:], lambda h, rb: (h, 0, 0, 0)),
        ],
        out_specs=pl.BlockSpec((tq, NA_HD), lambda h, rb: (rb, h)),
        out_shape=jax.ShapeDtypeStruct((s, NA_W), BF16),
        scratch_shapes=[pltpu.VMEM((s, NA_HD), BF16),
                        pltpu.VMEM((3, blk, NA_WIN_ROWS * GRID_W), F32)],
        compiler_params=pltpu.CompilerParams(
            dimension_semantics=("parallel", "arbitrary"),
            vmem_limit_bytes=VMEM_LIMIT),
        name="natten",
    )(u, u, u, qg.reshape(1, NA_HD), kg.reshape(1, NA_HD), tz)


def _merge_kernel(of_ref, ob_ref, hg_ref, nb_ref, ng_ref, ma0_ref, ma1_ref, mb0_ref, mb1_ref,
                  x_ref, p_ref, on_ref, pn_ref, gn_ref, wa_ref, wb_ref, wo_ref, wg_ref, wp_ref,
                  o_ref, *maybe_hn_ref):
    tm = x_ref.shape[0]
    subs = [slice(r, r + MERGE_SUB_ROWS) for r in range(0, tm, MERGE_SUB_ROWS)]

    def branch_a_input(rs):
        oa = of_ref[rs, :].astype(F32) + ob_ref[rs, :].astype(F32)
        gate_a = _silu(hg_ref[rs, :].astype(F32))
        parts = []
        for h in range(HG_HEADS):
            hs = slice(h * HG_DK, (h + 1) * HG_DK)
            oh = oa[:, hs]
            ms = jnp.mean(oh * oh, axis=-1, keepdims=True)
            parts.append((oh * lax.rsqrt(ms + EPS) * on_ref[...] * gate_a[:, hs]).astype(BF16))
        return jnp.concatenate(parts, axis=1)

    a_ins = [branch_a_input(rs) for rs in subs]
    yas = [_dot(a_in, wa_ref[...]) for a_in in a_ins]
    b_ins = [(nb_ref[rs, :].astype(F32) * _silu(ng_ref[rs, :].astype(F32))).astype(BF16) for rs in subs]
    ybs = [_dot(b_in, wb_ref[...]) for b_in in b_ins]
    ys = []
    for rs, ya, yb in zip(subs, yas, ybs):
        m_a = jnp.concatenate([ma0_ref[rs, :], ma1_ref[rs, :]], axis=1).astype(F32)
        m_b = jnp.concatenate([mb0_ref[rs, :], mb1_ref[rs, :]], axis=1).astype(F32)
        ys.append((_sigmoid(m_a) * ya + _sigmoid(m_b) * yb).astype(BF16))
    x1s = [x_ref[rs, :] + _dot(y, wo_ref[...]) for rs, y in zip(subs, ys)]
    gs = [_sigmoid(_dot(_rms_bf16(x1, pn_ref[...]), wg_ref[...])) for x1 in x1s]
    ples = [_dot(p_ref[rs, :].astype(BF16), wp_ref[...]) for rs in subs]
    for rs, x1, g, ple in zip(subs, x1s, gs, ples):
        x2 = x1 + ple * g
        o_ref[rs, :] = x2
        for hn_ref in maybe_hn_ref:
            hn_ref[rs, :] = _rms_bf16(x2, gn_ref[...])


def _merge(o_f, o_b, u, nb, x, p, p_index, onorm, pnorm, next_norm, weights, layer,
           cb_hgate, cb_ngate, cb_ma, tm=256):
    s, d = x.shape
    tm = min(tm, s)
    pd = p.shape[-1]
    act = lambda wdt: pl.BlockSpec((tm, wdt), lambda i: (i, 0))
    ucol = lambda cb: pl.BlockSpec((tm, HG_W), lambda i, cb=cb: (i, cb))
    vec = lambda n: pl.BlockSpec((1, n), lambda i: (0, 0), pipeline_mode=pl.Buffered(1))
    wspec = lambda w: pl.BlockSpec((None,) + w.shape[1:], lambda i: (layer, 0, 0),
                                   pipeline_mode=pl.Buffered(1))
    emit_next = next_norm is not None
    gn = next_norm if emit_next else pnorm
    out_specs = [act(d)] + ([act(d)] if emit_next else [])
    out_shape = [jax.ShapeDtypeStruct((s, d), F32)] + (
        [jax.ShapeDtypeStruct((s, d), BF16)] if emit_next else [])
    res = pl.pallas_call(
        _merge_kernel,
        grid=(s // tm,),
        in_specs=[act(HG_W), act(HG_W), ucol(cb_hgate), act(NA_W), ucol(cb_ngate),
                  ucol(cb_ma), ucol(cb_ma + 1), ucol(cb_ma + 2), ucol(cb_ma + 3),
                  act(d), pl.BlockSpec((None, tm, pd), lambda i: (p_index, i, 0)),
                  vec(HG_DK), vec(d), vec(d)] + [wspec(w) for w in weights],
        out_specs=out_specs,
        out_shape=out_shape,
        compiler_params=pltpu.CompilerParams(
            dimension_semantics=("parallel",),
            vmem_limit_bytes=VMEM_LIMIT),
        name="merge",
    )(o_f, o_b, u, nb, u, u, u, u, u, x, p, onorm.reshape(1, HG_DK), pnorm.reshape(1, d),
      gn.reshape(1, d), *weights)
    return (res[0], res[1]) if emit_next else (res[0], None)


def kernel(x, p, norm_g, w_in, hgrn_lb, hgrn_onorm, na_qnorm, na_knorm, na_rpb,
           w_branch_a, w_branch_b, w_out, ple_norm, w_ple_gate, w_ple):
    bsz, s, d = x.shape
    depth = w_in.shape[0]
    lbp = jax.nn.softmax(hgrn_lb.astype(F32), axis=0)
    lower = jnp.cumsum(lbp, axis=0) - lbp[0:1]
    weights = [w.astype(BF16) for w in (w_branch_a, w_branch_b, w_out, w_ple_gate, w_ple)]
    p_flat = p.reshape(depth * bsz, s, p.shape[-1])

    outs = []
    for bi in range(bsz):
        xb = x[bi]
        h = _rmsnorm(xb, norm_g[0])
        for i in range(depth):
            u = _in_proj(h, w_in, i)
            o_f, o_b = _hgrn(u, lower[i])
            nb = _natten(u, na_qnorm[i], na_knorm[i], _natten_toeplitz(na_rpb[i]),
                         col0=5 * HG_W // NA_HD)
            xb, h = _merge(o_f, o_b, u, nb, xb, p_flat, i * bsz + bi, hgrn_onorm[i], ple_norm[i],
                           norm_g[i + 1] if i + 1 < depth else None, weights, i,
                           cb_hgate=4, cb_ngate=8, cb_ma=9)
        outs.append(xb)
    return jnp.stack(outs, axis=0)
```

```python
import functools

import jax
import jax.numpy as jnp
import numpy as np
from jax import lax
from jax.experimental import pallas as pl
from jax.experimental.pallas import tpu as pltpu

F32 = jnp.float32
BF16 = jnp.bfloat16

GRID_W = 64
HG_HEADS = 8
HG_DK = 128
HG_W = HG_HEADS * HG_DK
NA_HEADS = 8
NA_HD = 128
NA_W = NA_HEADS * NA_HD
NA_KH = 8
NA_KW = 16
EPS = 1e-6
NEG = -1e30
F_MIN = 1e-6

CHUNK = 64
HG_CHUNKS_PER_STEP = 2
SUB = 8
NSUB = CHUNK // SUB
MID = SUB // 2

NA_ROWS_PER_STEP = 4
NA_WIN_ROWS = 12
NA_CHAINS = 4
LOG2E = 1.4426950408889634

VMEM_LIMIT = 56 * 1024 * 1024


def _dot(a, b):
    return jnp.dot(a, b, preferred_element_type=F32)


def _dot_nt(a, b):
    return lax.dot_general(a, b, (((1,), (1,)), ((), ())), preferred_element_type=F32)


def _sigmoid(x):
    return 1.0 / (1.0 + jnp.exp(-x))


def _silu(x):
    return x * _sigmoid(x)


def _rms_bf16(x, g):
    ms = jnp.mean(x * x, axis=-1, keepdims=True)
    return (x * lax.rsqrt(ms + EPS) * g).astype(BF16)


def _rmsnorm_kernel(x_ref, g_ref, o_ref):
    o_ref[...] = _rms_bf16(x_ref[...], g_ref[...])


def _rmsnorm(x, g, tm=512):
    s, d = x.shape
    tm = min(tm, s)
    return pl.pallas_call(
        _rmsnorm_kernel,
        grid=(s // tm,),
        in_specs=[pl.BlockSpec((tm, d), lambda i: (i, 0)),
                  pl.BlockSpec((1, d), lambda i: (0, 0))],
        out_specs=pl.BlockSpec((tm, d), lambda i: (i, 0)),
        out_shape=jax.ShapeDtypeStruct((s, d), BF16),
        compiler_params=pltpu.CompilerParams(
            dimension_semantics=("parallel",),
            vmem_limit_bytes=VMEM_LIMIT),
        name="rmsnorm",
    )(x, g.reshape(1, d))


def _inproj_kernel(h_ref, w_ref, o_ref, wb_ref):
    @pl.when(pl.program_id(1) == 0)
    def _():
        wb_ref[...] = w_ref[...].astype(BF16)

    o_ref[...] = _dot(h_ref[...], wb_ref[...]).astype(o_ref.dtype)


def _in_proj(h, w, layer, tm=2048, tn=1024):
    s, d = h.shape
    n = w.shape[2]
    tm = min(tm, s)
    return pl.pallas_call(
        _inproj_kernel,
        grid=(n // tn, s // tm),
        in_specs=[
            pl.BlockSpec((tm, d), lambda j, i: (i, 0)),
            pl.BlockSpec((None, d, tn), lambda j, i: (layer, 0, j)),
        ],
        out_specs=pl.BlockSpec((tm, tn), lambda j, i: (i, j)),
        out_shape=jax.ShapeDtypeStruct((s, n), BF16),
        scratch_shapes=[pltpu.VMEM((d, tn), BF16)],
        compiler_params=pltpu.CompilerParams(
            dimension_semantics=("parallel", "arbitrary"),
            vmem_limit_bytes=VMEM_LIMIT),
        name="in_proj",
    )(h, w)


def _hgrn_selectors():
    c = CHUNK
    t = np.arange(c)[:, None]
    r = np.arange(c)[None, :]
    m = (t // SUB) * SUB + MID
    out = []
    for backward in (False, True):
        if backward:
            cum = lambda p: (r >= p).astype(np.float32)
            prev_m = m - SUB
        else:
            cum = lambda p: (r <= p).astype(np.float32)
            prev_m = m + SUB
        has_prev = (prev_m >= 0) & (prev_m < c)
        step = np.where(has_prev, cum(np.clip(prev_m, 0, c - 1)) - cum(m), 0.0)
        total = np.ones((16, c), np.float32)
        sel = np.concatenate([cum(t) - cum(m), cum(m), 1.0 - cum(m), step, total], axis=0)
        out.append(np.concatenate([sel, sel], axis=1))
    return np.stack(out)


HG_SEL_ROWS = 4 * CHUNK + 16


def _hgrn_prologue(q_ref, v_ref, z_ref, rows, lb, sel, backward):
    c = CHUNK
    w = q_ref.shape[1]
    q = q_ref[rows, :].astype(F32)
    v = v_ref[rows, :]
    z = z_ref[rows, :].astype(F32)

    f = lb + (1.0 - lb) * _sigmoid(z)
    g = jnp.log2(jnp.maximum(f, F_MIN))
    k = 1.0 - f
    qs = q * _sigmoid(q)

    hi = g.astype(BF16)
    lo = (g - hi.astype(F32)).astype(BF16)
    cums = _dot(sel, jnp.concatenate([hi, lo], axis=0))
    e = jnp.exp2(cums[0:c])
    to_ref = jnp.exp2(cums[c:2 * c])
    from_ref = jnp.exp2(cums[2 * c:3 * c])
    step = jnp.exp2(cums[3 * c:4 * c])
    dec = jnp.exp2(cums[4 * c:4 * c + SUB])

    q0 = qs * e
    k0 = k * (1.0 / e)
    q_hat = (q0 * to_ref).astype(BF16)
    k_hat = (k0 * from_ref).astype(BF16)

    blocks = [slice(j * SUB, (j + 1) * SUB) for j in range(NSUB)]
    zero_blk = jnp.zeros((SUB, w), F32)
    order = range(NSUB) if backward else range(NSUB - 1, -1, -1)
    cur = [None] * NSUB
    q_tilde = [None] * NSUB
    k_tilde = [None] * NSUB
    for j in order:
        for jj in range(NSUB):
            if cur[jj] is not None:
                cur[jj] = cur[jj] * step[blocks[j]]
        cur[j] = q0[blocks[j]]
        q_tilde[j] = jnp.concatenate(
            [zero_blk if cur[jj] is None else cur[jj] for jj in range(NSUB)], axis=0).astype(BF16)
        k_tilde[j] = jnp.concatenate(
            [k0[blocks[jj]] if jj == j else zero_blk for jj in range(NSUB)], axis=0).astype(BF16)

    row = lax.broadcasted_iota(jnp.int32, (c, c), 0)
    col = lax.broadcasted_iota(jnp.int32, (c, c), 1)
    causal = (row >= col) if backward else (row <= col)
    heads = []
    for h in range(w // HG_DK):
        hs = slice(h * HG_DK, (h + 1) * HG_DK)
        heads.append(dict(
            q_hat=q_hat[:, hs], k_hat=k_hat[:, hs], v=v[:, hs], dec=dec[:, hs],
            q_cat=jnp.concatenate([q_tilde[j][:, hs] for j in range(NSUB)], axis=1),
            k_cat=jnp.concatenate([k_tilde[j][:, hs] for j in range(NSUB)], axis=1)))
    return heads, causal


def _hgrn_scores_t(q_cat, k_cat):
    rt = 2 * SUB
    kw = 2 * HG_DK
    return jnp.concatenate(
        [_dot_nt(k_cat[p * rt:(p + 1) * rt, p * kw:(p + 1) * kw], q_cat[:, p * kw:(p + 1) * kw])
         for p in range(CHUNK // rt)], axis=0)


def _hgrn_kernel(*refs, n_cast):
    (qf_ref, vf_ref, zf_ref, qb_ref, vb_ref, zb_ref, lb_ref, sel_ref), refs = refs[:8], refs[8:]
    w32_refs, refs = refs[:n_cast], refs[n_cast:]
    (of_ref, ob_ref), refs = refs[:2], refs[2:]
    w16_refs, (stf_ref, stb_ref) = refs[:n_cast], refs[n_cast:]

    @pl.when(pl.program_id(0) == 0)
    def _():
        stf_ref[...] = jnp.zeros_like(stf_ref)
        stb_ref[...] = jnp.zeros_like(stb_ref)

    for w32_ref, w16_ref in zip(w32_refs, w16_refs):
        w16_ref[...] = w32_ref[...].astype(w16_ref.dtype)

    n_heads = stf_ref.shape[0]
    n_chunks = qf_ref.shape[0] // CHUNK
    rows_f = [slice(c * CHUNK, (c + 1) * CHUNK) for c in range(n_chunks)]
    rows_b = rows_f[::-1]
    pro = [(_hgrn_prologue(qf_ref, vf_ref, zf_ref, rf, lb_ref[0:1, :], sel_ref[0], False),
            _hgrn_prologue(qb_ref, vb_ref, zb_ref, rb, lb_ref[1:2, :], sel_ref[1], True))
           for rf, rb in zip(rows_f, rows_b)]

    states = [stf_ref[h] for h in range(n_heads)] + [stb_ref[h] for h in range(n_heads)]
    for (rf, rb), ((heads_f, causal_f), (heads_b, causal_b)) in zip(zip(rows_f, rows_b), pro):
        work = ([(heads_f[h], causal_f, of_ref, rf, h) for h in range(n_heads)]
                + [(heads_b[h], causal_b, ob_ref, rb, h) for h in range(n_heads)])
        atts = [_hgrn_scores_t(a["q_cat"], a["k_cat"]) for (a, _, _, _, _) in work]
        updates = [_dot(a["v"].T, a["k_hat"]) for (a, _, _, _, _) in work]
        atts = [jnp.where(causal, att, 0.0).astype(BF16).T
                for att, (_, causal, _, _, _) in zip(atts, work)]
        outs = [_dot(jnp.concatenate([a["q_hat"], att], axis=1),
                     jnp.concatenate([st.astype(BF16).T, a["v"]], axis=0))
                for att, st, (a, _, _, _, _) in zip(atts, states, work)]
        new_states = []
        for (a, _, o_ref, rows, h), st, out, upd in zip(work, states, outs, updates):
            o_ref[rows, h * HG_DK:(h + 1) * HG_DK] = out.astype(o_ref.dtype)
            decayed = st.reshape(HG_DK // SUB, SUB, HG_DK) * a["dec"][None]
            new_states.append(decayed.reshape(HG_DK, HG_DK) + upd)
        states = new_states
    for h in range(n_heads):
        stf_ref[h] = states[h]
        stb_ref[h] = states[n_heads + h]


def _hgrn(u, lb, cast_weights=(), layer=0):
    s = u.shape[0]
    t = min(HG_CHUNKS_PER_STEP * CHUNK, s)
    n = s // t
    fwd = lambda cb: pl.BlockSpec((t, HG_W), lambda i, cb=cb: (i, cb))
    bwd = lambda cb: pl.BlockSpec((t, HG_W), lambda i, cb=cb: (n - 1 - i, cb))
    heads = HG_W // HG_DK
    cast_rows = [w.shape[1] // n for w in cast_weights]
    assert all(w.shape[1] == r * n and r % 16 == 0 for w, r in zip(cast_weights, cast_rows))
    res = pl.pallas_call(
        functools.partial(_hgrn_kernel, n_cast=len(cast_weights)),
        grid=(n,),
        in_specs=[fwd(0), fwd(1), fwd(2), bwd(0), bwd(1), bwd(3),
                  pl.BlockSpec((2, HG_W), lambda i: (0, 0)),
                  pl.BlockSpec((2, HG_SEL_ROWS, 2 * CHUNK), lambda i: (0, 0, 0))]
                 + [pl.BlockSpec((None, r, w.shape[2]), lambda i: (layer, i, 0))
                    for w, r in zip(cast_weights, cast_rows)],
        out_specs=[pl.BlockSpec((t, HG_W), lambda i: (i, 0)),
                   pl.BlockSpec((t, HG_W), lambda i: (n - 1 - i, 0))]
                  + [pl.BlockSpec((r, w.shape[2]), lambda i: (i, 0))
                     for w, r in zip(cast_weights, cast_rows)],
        out_shape=[jax.ShapeDtypeStruct((s, HG_W), BF16)] * 2
                  + [jax.ShapeDtypeStruct(w.shape[1:], BF16) for w in cast_weights],
        scratch_shapes=[pltpu.VMEM((heads, HG_DK, HG_DK), F32)] * 2,
        compiler_params=pltpu.CompilerParams(
            dimension_semantics=("arbitrary",),
            vmem_limit_bytes=VMEM_LIMIT),
        name="hgrn",
    )(u, u, u, u, u, u, lb, jnp.asarray(_hgrn_selectors(), BF16), *cast_weights)
    return res[0], res[1], list(res[2:])


NA_NEG_BLOCK = 2 * NA_KH - 1


def _natten_fill_bias(tz_ref, bias_ref, w0, r0, rows):
    kh = min(NA_KH, rows)
    lane = lax.broadcasted_iota(jnp.int32, (GRID_W, 2 * GRID_W), 1)

    def block_index(qi, a):
        r = r0 + qi
        r_start = min(max(r - kh // 2, 0), rows - kh)
        ar = w0 + a
        if r_start <= ar < r_start + kh:
            return ar - r + (NA_KH - 1)
        return NA_NEG_BLOCK

    for qi in range(NA_ROWS_PER_STEP):
        for ap in range(NA_WIN_ROWS // 2):
            ie, io = block_index(qi, 2 * ap), block_index(qi, 2 * ap + 1)
            blk = tz_ref[0, ie]
            if io != ie:
                blk = jnp.where(lane < GRID_W, blk, tz_ref[0, io])
            bias_ref[qi * GRID_W:(qi + 1) * GRID_W, ap * 2 * GRID_W:(ap + 1) * 2 * GRID_W] = blk


def _natten_kernel(q_ref, k_ref, v_ref, gate_ref, qg_ref, kg_ref, tz_ref, o_ref, kn_ref, bias_ref):
    step = pl.program_id(1)
    s = k_ref.shape[0]
    rows = s // GRID_W
    rq, rk = NA_ROWS_PER_STEP, NA_WIN_ROWS
    n_blocks = rows // rq
    tq, win = rq * GRID_W, rk * GRID_W
    kblk = 512

    @pl.when(step == 0)
    def _():
        def body(i, carry):
            sl = pl.ds(pl.multiple_of(i * kblk, kblk), kblk)
            kk = k_ref[sl, :].astype(F32)
            ms = jnp.mean(kk * kk, axis=-1, keepdims=True)
            kn_ref[sl, :] = (kk * lax.rsqrt(ms + EPS) * kg_ref[...]).astype(BF16)
            return carry
        lax.fori_loop(0, s // kblk, body, 0)
        _natten_fill_bias(tz_ref, bias_ref.at[0], 0, 0, rows)
        _natten_fill_bias(tz_ref, bias_ref.at[1], max(rq - NA_KH // 2, 0), rq, rows)
        _natten_fill_bias(tz_ref, bias_ref.at[2], rows - rk, rows - rq, rows)

    q = q_ref[...].astype(F32)
    ms = jnp.mean(q * q, axis=-1, keepdims=True)
    qn = (q * lax.rsqrt(ms + EPS) * (qg_ref[...] * (NA_HD ** -0.5 * LOG2E))).astype(BF16)

    qs, kws, vws, variants = [], [], [], []
    for c in range(NA_CHAINS):
        rb = step * NA_CHAINS + c
        w0 = jnp.clip(rq * rb - NA_KH // 2, 0, rows - rk) * GRID_W
        w0 = pl.multiple_of(w0, GRID_W)
        qs.append(qn[c * tq:(c + 1) * tq])
        kws.append(kn_ref[pl.ds(w0, win), :])
        vws.append(v_ref[pl.ds(w0, win), :])
        variants.append(jnp.where(rb == 0, 0, jnp.where(rb == n_blocks - 1, 2, 1)))
    scs = [_dot_nt(qc, kw) for qc, kw in zip(qs, kws)]
    scs = [sc + bias_ref[var] for sc, var in zip(scs, variants)]
    ms = [jnp.max(sc, axis=-1, keepdims=True) for sc in scs]
    ps = [jnp.exp2(sc - m) for sc, m in zip(scs, ms)]
    ls = [jnp.sum(p, axis=-1, keepdims=True) for p in ps]
    outs = [_dot(p.astype(BF16), vw) for p, vw in zip(ps, vws)]
    for c, (o, l) in enumerate(zip(outs, ls)):
        gate = _silu(gate_ref[c * tq:(c + 1) * tq, :].astype(F32))
        o_ref[c * tq:(c + 1) * tq, :] = (o / l * gate).astype(o_ref.dtype)


def _natten_toeplitz(rpb):
    c = np.arange(GRID_W)
    c_start = np.clip(c - NA_KW // 2, 0, GRID_W - NA_KW)
    col_in = (c[None, :] >= c_start[:, None]) & (c[None, :] < c_start[:, None] + NA_KW)
    col_off = np.clip(c[None, :] - c[:, None] + (NA_KW - 1), 0, 2 * NA_KW - 2)
    sel = ((col_off[..., None] == np.arange(2 * NA_KW - 1)) & col_in[..., None]).astype(np.float32)
    tz = jnp.einsum('hij,ckj->hick', rpb.astype(F32), sel, precision=lax.Precision.HIGHEST)
    tz = jnp.where(col_in[None, None], tz * LOG2E, NEG)
    tz = jnp.concatenate([tz, jnp.full_like(tz[:, :1], NEG)], axis=1)
    return jnp.concatenate([tz, tz], axis=-1)


def _natten(u, qg, kg, tz, col0):
    s = u.shape[0]
    blk = NA_ROWS_PER_STEP * GRID_W
    tq = NA_CHAINS * blk
    n_steps = s // tq
    assert s % tq == 0 and s // blk >= 3 and s // GRID_W >= NA_WIN_ROWS
    return pl.pallas_call(
        _natten_kernel,
        grid=(NA_HEADS, n_steps),
        in_specs=[
            pl.BlockSpec((tq, NA_HD), lambda h, rb: (rb, col0 + h)),
            pl.BlockSpec((s, NA_HD), lambda h, rb: (0, col0 + NA_HEADS + h)),
            pl.BlockSpec((s, NA_HD), lambda h, rb: (0, col0 + 2 * NA_HEADS + h)),
            pl.BlockSpec((tq, NA_HD), lambda h, rb: (rb, col0 + 3 * NA_HEADS + h)),
            pl.BlockSpec((1, NA_HD), lambda h, rb: (0, 0)),
            pl.BlockSpec((1, NA_HD), lambda h, rb: (0, 0)),
            pl.BlockSpec((1,) + tz.shape[1:], lambda h, rb: (h, 0, 0, 0)),
        ],
        out_specs=pl.BlockSpec((tq, NA_HD), lambda h, rb: (rb, h)),
        out_shape=jax.ShapeDtypeStruct((s, NA_W), BF16),
        scratch_shapes=[pltpu.VMEM((s, NA_HD), BF16),
                        pltpu.VMEM((3, blk, NA_WIN_ROWS * GRID_W), F32)],
        compiler_params=pltpu.CompilerParams(
            dimension_semantics=("parallel", "arbitrary"),
            vmem_limit_bytes=VMEM_LIMIT),
        name="natten",
    )(u, u, u, u, qg.reshape(1, NA_HD), kg.reshape(1, NA_HD), tz)


def _merge_kernel(of_ref, ob_ref, hg_ref, bin_ref, ma0_ref, ma1_ref, mb0_ref, mb1_ref,
                  x_ref, p_ref, on_ref, pn_ref, gn_ref, wa_ref, wb_ref, wo_ref, wg_ref, wp_ref,
                  o_ref, *maybe_hn_ref):
    oa = of_ref[...].astype(F32) + ob_ref[...].astype(F32)
    gate_a = _silu(hg_ref[...].astype(F32))
    parts = []
    for h in range(HG_HEADS):
        hs = slice(h * HG_DK, (h + 1) * HG_DK)
        oh = oa[:, hs]
        ms = jnp.mean(oh * oh, axis=-1, keepdims=True)
        parts.append((oh * lax.rsqrt(ms + EPS) * on_ref[...] * gate_a[:, hs]).astype(BF16))
    ya = _dot(jnp.concatenate(parts, axis=1), wa_ref[...])
    yb = _dot(bin_ref[...], wb_ref[...])
    m_a = jnp.concatenate([ma0_ref[...], ma1_ref[...]], axis=1).astype(F32)
    m_b = jnp.concatenate([mb0_ref[...], mb1_ref[...]], axis=1).astype(F32)
    y = _sigmoid(m_a) * ya + _sigmoid(m_b) * yb
    x1 = x_ref[...] + _dot(y.astype(BF16), wo_ref[...])
    g = _sigmoid(_dot(_rms_bf16(x1, pn_ref[...]), wg_ref[...]))
    x2 = x1 + _dot(p_ref[...].astype(BF16), wp_ref[...]) * g
    o_ref[...] = x2
    for hn_ref in maybe_hn_ref:
        hn_ref[...] = _rms_bf16(x2, gn_ref[...])


def _merge(o_f, o_b, u, b_in, x, p, p_index, onorm, pnorm, next_norm, weights, cb_hgate, cb_ma, tm=256):
    s, d = x.shape
    tm = min(tm, s)
    pd = p.shape[-1]
    act = lambda wdt: pl.BlockSpec((tm, wdt), lambda i: (i, 0))
    ucol = lambda cb: pl.BlockSpec((tm, HG_W), lambda i, cb=cb: (i, cb))
    const = lambda shp: pl.BlockSpec(shp, lambda i: (0, 0), pipeline_mode=pl.Buffered(1))
    emit_next = next_norm is not None
    gn = next_norm if emit_next else pnorm
    out_specs = [act(d)] + ([act(d)] if emit_next else [])
    out_shape = [jax.ShapeDtypeStruct((s, d), F32)] + (
        [jax.ShapeDtypeStruct((s, d), BF16)] if emit_next else [])
    res = pl.pallas_call(
        _merge_kernel,
        grid=(s // tm,),
        in_specs=[act(HG_W), act(HG_W), ucol(cb_hgate), act(NA_W),
                  ucol(cb_ma), ucol(cb_ma + 1), ucol(cb_ma + 2), ucol(cb_ma + 3),
                  act(d), pl.BlockSpec((None, tm, pd), lambda i: (p_index, i, 0)),
                  const((1, HG_DK)), const((1, d)), const((1, d))]
                 + [const(w.shape) for w in weights],
        out_specs=out_specs,
        out_shape=out_shape,
        compiler_params=pltpu.CompilerParams(
            dimension_semantics=("parallel",),
            vmem_limit_bytes=VMEM_LIMIT),
        name="merge",
    )(o_f, o_b, u, b_in, u, u, u, u, x, p, onorm.reshape(1, HG_DK), pnorm.reshape(1, d),
      gn.reshape(1, d), *weights)
    return (res[0], res[1]) if emit_next else (res[0], None)


def kernel(x, p, norm_g, w_in, hgrn_lb, hgrn_onorm, na_qnorm, na_knorm, na_rpb,
           w_branch_a, w_branch_b, w_out, ple_norm, w_ple_gate, w_ple):
    bsz, s, d = x.shape
    depth = w_in.shape[0]
    lbp = jax.nn.softmax(hgrn_lb.astype(F32), axis=0)
    lower = jnp.cumsum(lbp, axis=0) - lbp[0:1]
    p_flat = p.reshape(depth * bsz, s, p.shape[-1])
    cb_hgate = 4 * HG_W // HG_W
    cb_na = 5 * HG_W // NA_HD
    cb_ma = (5 * HG_W + 4 * NA_W) // HG_W

    outs = []
    for bi in range(bsz):
        xb = x[bi]
        h = _rmsnorm(xb, norm_g[0])
        for i in range(depth):
            u = _in_proj(h, w_in, i)
            o_f, o_b, weights = _hgrn(u, lower[i], (w_branch_a, w_branch_b, w_out, w_ple_gate), i)
            b_in = _natten(u, na_qnorm[i], na_knorm[i], _natten_toeplitz(na_rpb[i]), col0=cb_na)
            xb, h = _merge(o_f, o_b, u, b_in, xb, p_flat, i * bsz + bi, hgrn_onorm[i], ple_norm[i],
                           norm_g[i + 1] if i + 1 < depth else None,
                           weights + [w_ple[i].astype(BF16)], cb_hgate=cb_hgate, cb_ma=cb_ma)
        outs.append(xb)
    return jnp.stack(outs, axis=0)
```

```python
import functools

import jax
import jax.numpy as jnp
import numpy as np
from jax import lax
from jax.experimental import pallas as pl
from jax.experimental.pallas import tpu as pltpu

F32 = jnp.float32
BF16 = jnp.bfloat16

GRID_W = 64
HG_HEADS = 8
HG_DK = 128
HG_W = HG_HEADS * HG_DK
NA_HEADS = 8
NA_HD = 128
NA_W = NA_HEADS * NA_HD
NA_KH = 8
NA_KW = 16
EPS = 1e-6
NEG = -1e30
F_MIN = 1e-6

CHUNK = 64
HG_CHUNKS_PER_STEP = 4
SUB = 8
NSUB = CHUNK // SUB
MID = SUB // 2

NA_ROWS_PER_STEP = 4
NA_WIN_ROWS = 12
NA_CHAINS = 8
NA_KT_TOKENS = 256
LOG2E = 1.4426950408889634

VMEM_LIMIT = 56 * 1024 * 1024


def _dot(a, b):
    return jnp.dot(a, b, preferred_element_type=F32)


def _dot_nt(a, b):
    return lax.dot_general(a, b, (((1,), (1,)), ((), ())), preferred_element_type=F32)


def _sigmoid(x):
    return 1.0 / (1.0 + jnp.exp(-x))


def _silu(x):
    return x * _sigmoid(x)


def _rms_bf16(x, g):
    ms = jnp.mean(x * x, axis=-1, keepdims=True)
    return (x * lax.rsqrt(ms + EPS) * g).astype(BF16)


def _rmsnorm_kernel(x_ref, g_ref, o_ref):
    o_ref[...] = _rms_bf16(x_ref[...], g_ref[...])


def _rmsnorm(x, g, tm=512):
    s, d = x.shape
    tm = min(tm, s)
    return pl.pallas_call(
        _rmsnorm_kernel,
        grid=(s // tm,),
        in_specs=[pl.BlockSpec((tm, d), lambda i: (i, 0)),
                  pl.BlockSpec((1, d), lambda i: (0, 0))],
        out_specs=pl.BlockSpec((tm, d), lambda i: (i, 0)),
        out_shape=jax.ShapeDtypeStruct((s, d), BF16),
        compiler_params=pltpu.CompilerParams(
            dimension_semantics=("parallel",),
            vmem_limit_bytes=VMEM_LIMIT),
        name="rmsnorm",
    )(x, g.reshape(1, d))


def _inproj_kernel(h_ref, w_ref, o_ref, wb_ref):
    @pl.when(pl.program_id(1) == 0)
    def _():
        wb_ref[...] = w_ref[...].astype(BF16)

    o_ref[...] = _dot(h_ref[...], wb_ref[...]).astype(o_ref.dtype)


def _in_proj(h, w, layer, tm=2048, tn=1024):
    s, d = h.shape
    n = w.shape[2]
    tm = min(tm, s)
    return pl.pallas_call(
        _inproj_kernel,
        grid=(n // tn, s // tm),
        in_specs=[
            pl.BlockSpec((tm, d), lambda j, i: (i, 0)),
            pl.BlockSpec((None, d, tn), lambda j, i: (layer, 0, j)),
        ],
        out_specs=pl.BlockSpec((tm, tn), lambda j, i: (i, j)),
        out_shape=jax.ShapeDtypeStruct((s, n), BF16),
        scratch_shapes=[pltpu.VMEM((d, tn), BF16)],
        compiler_params=pltpu.CompilerParams(
            dimension_semantics=("parallel", "arbitrary"),
            vmem_limit_bytes=VMEM_LIMIT),
        name="in_proj",
    )(h, w)


def _hgrn_selectors():
    c = CHUNK
    t = np.arange(c)[:, None]
    r = np.arange(c)[None, :]
    m = (t // SUB) * SUB + MID
    out = []
    for backward in (False, True):
        if backward:
            cum = lambda p: (r >= p).astype(np.float32)
            prev_m = m - SUB
        else:
            cum = lambda p: (r <= p).astype(np.float32)
            prev_m = m + SUB
        has_prev = (prev_m >= 0) & (prev_m < c)
        step = np.where(has_prev, cum(np.clip(prev_m, 0, c - 1)) - cum(m), 0.0)
        total = np.ones((16, c), np.float32)
        sel = np.concatenate([cum(t) - cum(m), cum(m), 1.0 - cum(m), step, total], axis=0)
        out.append(np.concatenate([sel, sel], axis=1))
    return np.stack(out)


HG_SEL_ROWS = 4 * CHUNK + 16


def _hgrn_prologue(q_ref, v_ref, z_ref, rows, lb, sel, backward):
    c = CHUNK
    w = q_ref.shape[1]
    q = q_ref[rows, :].astype(F32)
    v = v_ref[rows, :]
    z = z_ref[rows, :].astype(F32)

    f = lb + (1.0 - lb) * _sigmoid(z)
    g = jnp.log2(jnp.maximum(f, F_MIN))
    k = 1.0 - f
    qs = q * _sigmoid(q)

    hi = g.astype(BF16)
    lo = (g - hi.astype(F32)).astype(BF16)
    cums = _dot(sel, jnp.concatenate([hi, lo], axis=0))
    e = jnp.exp2(cums[0:c])
    to_ref = jnp.exp2(cums[c:2 * c])
    from_ref = jnp.exp2(cums[2 * c:3 * c])
    step = jnp.exp2(cums[3 * c:4 * c])
    dec = jnp.exp2(cums[4 * c:4 * c + SUB])

    q0 = qs * e
    k0 = k * (1.0 / e)
    q_hat = (q0 * to_ref).astype(BF16)
    k_hat = (k0 * from_ref).astype(BF16)

    blocks = [slice(j * SUB, (j + 1) * SUB) for j in range(NSUB)]
    zero_blk = jnp.zeros((SUB, w), F32)
    order = range(NSUB) if backward else range(NSUB - 1, -1, -1)
    cur = [None] * NSUB
    q_tilde = [None] * NSUB
    k_tilde = [None] * NSUB
    for j in order:
        for jj in range(NSUB):
            if cur[jj] is not None:
                cur[jj] = cur[jj] * step[blocks[j]]
        cur[j] = q0[blocks[j]]
        q_tilde[j] = jnp.concatenate(
            [zero_blk if cur[jj] is None else cur[jj] for jj in range(NSUB)], axis=0).astype(BF16)
        k_tilde[j] = jnp.concatenate(
            [k0[blocks[jj]] if jj == j else zero_blk for jj in range(NSUB)], axis=0).astype(BF16)

    row = lax.broadcasted_iota(jnp.int32, (c, c), 0)
    col = lax.broadcasted_iota(jnp.int32, (c, c), 1)
    causal = (row >= col) if backward else (row <= col)
    heads = []
    for h in range(w // HG_DK):
        hs = slice(h * HG_DK, (h + 1) * HG_DK)
        heads.append(dict(
            q_hat=q_hat[:, hs], k_hat=k_hat[:, hs], v=v[:, hs], dec=dec[:, hs],
            q_cat=jnp.concatenate([q_tilde[j][:, hs] for j in range(NSUB)], axis=1),
            k_cat=jnp.concatenate([k_tilde[j][:, hs] for j in range(NSUB)], axis=1)))
    return heads, causal


def _hgrn_scores_t(q_cat, k_cat):
    rt = 2 * SUB
    kw = 2 * HG_DK
    return jnp.concatenate(
        [_dot_nt(k_cat[p * rt:(p + 1) * rt, p * kw:(p + 1) * kw], q_cat[:, p * kw:(p + 1) * kw])
         for p in range(CHUNK // rt)], axis=0)


def _hgrn_kernel(*refs, n_cast):
    (qf_ref, vf_ref, zf_ref, qb_ref, vb_ref, zb_ref, lb_ref, sel_ref, nk_ref, kg_ref) = refs[:10]
    refs = refs[10:]
    w32_refs, refs = refs[:n_cast], refs[n_cast:]
    (of_ref, ob_ref, knt_ref), refs = refs[:3], refs[3:]
    w16_refs, (stf_ref, stb_ref) = refs[:n_cast], refs[n_cast:]

    @pl.when(pl.program_id(0) == 0)
    def _():
        stf_ref[...] = jnp.zeros_like(stf_ref)
        stb_ref[...] = jnp.zeros_like(stb_ref)

    for w32_ref, w16_ref in zip(w32_refs, w16_refs):
        w16_ref[...] = w32_ref[...].astype(w16_ref.dtype)
    for h in range(knt_ref.shape[0]):
        kk = nk_ref[:, h * NA_HD:(h + 1) * NA_HD].astype(F32)
        knt_ref[h, 0] = _rms_bf16(kk, kg_ref[...]).T

    n_heads = stf_ref.shape[0]
    n_chunks = qf_ref.shape[0] // CHUNK
    rows_f = [slice(c * CHUNK, (c + 1) * CHUNK) for c in range(n_chunks)]
    rows_b = rows_f[::-1]
    pro = [(_hgrn_prologue(qf_ref, vf_ref, zf_ref, rf, lb_ref[0:1, :], sel_ref[0], False),
            _hgrn_prologue(qb_ref, vb_ref, zb_ref, rb, lb_ref[1:2, :], sel_ref[1], True))
           for rf, rb in zip(rows_f, rows_b)]

    states = [stf_ref[h] for h in range(n_heads)] + [stb_ref[h] for h in range(n_heads)]
    for (rf, rb), ((heads_f, causal_f), (heads_b, causal_b)) in zip(zip(rows_f, rows_b), pro):
        work = ([(heads_f[h], causal_f, of_ref, rf, h) for h in range(n_heads)]
                + [(heads_b[h], causal_b, ob_ref, rb, h) for h in range(n_heads)])
        atts = [_hgrn_scores_t(a["q_cat"], a["k_cat"]) for (a, _, _, _, _) in work]
        updates = [_dot(a["v"].T, a["k_hat"]) for (a, _, _, _, _) in work]
        atts = [jnp.where(causal, att, 0.0).astype(BF16).T
                for att, (_, causal, _, _, _) in zip(atts, work)]
        outs = [_dot(jnp.concatenate([a["q_hat"], att], axis=1),
                     jnp.concatenate([st.astype(BF16).T, a["v"]], axis=0))
                for att, st, (a, _, _, _, _) in zip(atts, states, work)]
        new_states = []
        for (a, _, o_ref, rows, h), st, out, upd in zip(work, states, outs, updates):
            o_ref[rows, h * HG_DK:(h + 1) * HG_DK] = out.astype(o_ref.dtype)
            decayed = st.reshape(HG_DK // SUB, SUB, HG_DK) * a["dec"][None]
            new_states.append(decayed.reshape(HG_DK, HG_DK) + upd)
        states = new_states
    for h in range(n_heads):
        stf_ref[h] = states[h]
        stb_ref[h] = states[n_heads + h]


def _hgrn(u, lb, kg, cb_nk, cast_weights=(), layer=0):
    s = u.shape[0]
    t = HG_CHUNKS_PER_STEP * CHUNK
    n = s // t
    assert s % t == 0 and NA_KT_TOKENS % t == 0 and t % 128 == 0
    per_tile = NA_KT_TOKENS // t
    fwd = lambda cb: pl.BlockSpec((t, HG_W), lambda i, cb=cb: (i, cb))
    bwd = lambda cb: pl.BlockSpec((t, HG_W), lambda i, cb=cb: (n - 1 - i, cb))
    heads = HG_W // HG_DK
    cast_rows = [w.shape[1] // n for w in cast_weights]
    assert all(w.shape[1] == r * n and r % 16 == 0 for w, r in zip(cast_weights, cast_rows))
    res = pl.pallas_call(
        functools.partial(_hgrn_kernel, n_cast=len(cast_weights)),
        grid=(n,),
        in_specs=[fwd(0), fwd(1), fwd(2), bwd(0), bwd(1), bwd(3),
                  pl.BlockSpec((2, HG_W), lambda i: (0, 0)),
                  pl.BlockSpec((2, HG_SEL_ROWS, 2 * CHUNK), lambda i: (0, 0, 0)),
                  pl.BlockSpec((t, NA_W), lambda i: (i, cb_nk)),
                  pl.BlockSpec((1, NA_HD), lambda i: (0, 0))]
                 + [pl.BlockSpec((None, r, w.shape[2]), lambda i: (layer, i, 0))
                    for w, r in zip(cast_weights, cast_rows)],
        out_specs=[pl.BlockSpec((t, HG_W), lambda i: (i, 0)),
                   pl.BlockSpec((t, HG_W), lambda i: (n - 1 - i, 0)),
                   pl.BlockSpec((NA_HEADS, 1, NA_HD, t), lambda i: (0, i // per_tile, 0, i % per_tile))]
                  + [pl.BlockSpec((r, w.shape[2]), lambda i: (i, 0))
                     for w, r in zip(cast_weights, cast_rows)],
        out_shape=[jax.ShapeDtypeStruct((s, HG_W), BF16)] * 2
                  + [jax.ShapeDtypeStruct((NA_HEADS, s // NA_KT_TOKENS, NA_HD, NA_KT_TOKENS), BF16)]
                  + [jax.ShapeDtypeStruct(w.shape[1:], BF16) for w in cast_weights],
        scratch_shapes=[pltpu.VMEM((heads, HG_DK, HG_DK), F32)] * 2,
        compiler_params=pltpu.CompilerParams(
            dimension_semantics=("arbitrary",),
            vmem_limit_bytes=VMEM_LIMIT),
        name="hgrn",
    )(u, u, u, u, u, u, lb, jnp.asarray(_hgrn_selectors(), BF16), u, kg.reshape(1, NA_HD),
      *cast_weights)
    return res[0], res[1], res[2], list(res[3:])


NA_NEG_BLOCK = 2 * NA_KH - 1


def _natten_fill_bias(tz_ref, bias_ref, w0, r0, rows):
    kh = min(NA_KH, rows)
    lane = lax.broadcasted_iota(jnp.int32, (GRID_W, 2 * GRID_W), 1)

    def block_index(qi, a):
        r = r0 + qi
        r_start = min(max(r - kh // 2, 0), rows - kh)
        ar = w0 + a
        if r_start <= ar < r_start + kh:
            return ar - r + (NA_KH - 1)
        return NA_NEG_BLOCK

    for qi in range(NA_ROWS_PER_STEP):
        for ap in range(NA_WIN_ROWS // 2):
            ie, io = block_index(qi, 2 * ap), block_index(qi, 2 * ap + 1)
            blk = tz_ref[0, ie]
            if io != ie:
                blk = jnp.where(lane < GRID_W, blk, tz_ref[0, io])
            bias_ref[qi * GRID_W:(qi + 1) * GRID_W, ap * 2 * GRID_W:(ap + 1) * 2 * GRID_W] = blk


def _natten_kernel(q_ref, knt_ref, v_ref, gate_ref, qg_ref, tz_ref, o_ref, bias_ref):
    step = pl.program_id(1)
    s = v_ref.shape[0]
    rows = s // GRID_W
    rq, rk = NA_ROWS_PER_STEP, NA_WIN_ROWS
    n_blocks = rows // rq
    tq, win = rq * GRID_W, rk * GRID_W
    win_tiles = win // NA_KT_TOKENS

    @pl.when(step == 0)
    def _():
        _natten_fill_bias(tz_ref, bias_ref.at[0], 0, 0, rows)
        _natten_fill_bias(tz_ref, bias_ref.at[1], max(rq - NA_KH // 2, 0), rq, rows)
        _natten_fill_bias(tz_ref, bias_ref.at[2], rows - rk, rows - rq, rows)

    q = q_ref[...].astype(F32)
    ms = jnp.mean(q * q, axis=-1, keepdims=True)
    qn = (q * lax.rsqrt(ms + EPS) * (qg_ref[...] * (NA_HD ** -0.5 * LOG2E))).astype(BF16)

    qs, kws, vws, variants = [], [], [], []
    for c in range(NA_CHAINS):
        rb = step * NA_CHAINS + c
        w0 = jnp.clip(rq * rb - NA_KH // 2, 0, rows - rk) * GRID_W
        w0 = pl.multiple_of(w0, GRID_W)
        qs.append(qn[c * tq:(c + 1) * tq])
        kt = knt_ref[pl.ds(w0 // NA_KT_TOKENS, win_tiles)]
        kws.append(jnp.concatenate([kt[i] for i in range(win_tiles)], axis=1))
        vws.append(v_ref[pl.ds(w0, win), :])
        variants.append(jnp.where(rb == 0, 0, jnp.where(rb == n_blocks - 1, 2, 1)))
    scs = [_dot(qc, kw) for qc, kw in zip(qs, kws)]
    scs = [sc + bias_ref[var] for sc, var in zip(scs, variants)]
    ms = [jnp.max(sc, axis=-1, keepdims=True) for sc in scs]
    ps = [jnp.exp2(sc - m) for sc, m in zip(scs, ms)]
    ls = [jnp.sum(p, axis=-1, keepdims=True) for p in ps]
    outs = [_dot(p.astype(BF16), vw) for p, vw in zip(ps, vws)]
    for c, (o, l) in enumerate(zip(outs, ls)):
        gate = _silu(gate_ref[c * tq:(c + 1) * tq, :].astype(F32))
        o_ref[c * tq:(c + 1) * tq, :] = (o / l * gate).astype(o_ref.dtype)


def _natten_toeplitz(rpb):
    c = np.arange(GRID_W)
    c_start = np.clip(c - NA_KW // 2, 0, GRID_W - NA_KW)
    col_in = (c[None, :] >= c_start[:, None]) & (c[None, :] < c_start[:, None] + NA_KW)
    col_off = np.clip(c[None, :] - c[:, None] + (NA_KW - 1), 0, 2 * NA_KW - 2)
    sel = ((col_off[..., None] == np.arange(2 * NA_KW - 1)) & col_in[..., None]).astype(np.float32)
    tz = jnp.einsum('hij,ckj->hick', rpb.astype(F32), sel, precision=lax.Precision.HIGHEST)
    tz = jnp.where(col_in[None, None], tz * LOG2E, NEG)
    tz = jnp.concatenate([tz, jnp.full_like(tz[:, :1], NEG)], axis=1)
    return jnp.concatenate([tz, tz], axis=-1)


def _natten(u, knt, qg, tz, col0):
    s = u.shape[0]
    blk = NA_ROWS_PER_STEP * GRID_W
    tq = NA_CHAINS * blk
    n_steps = s // tq
    assert s % tq == 0 and s // blk >= 3 and s // GRID_W >= NA_WIN_ROWS
    assert blk % NA_KT_TOKENS == 0 and (NA_KH // 2 * GRID_W) % NA_KT_TOKENS == 0
    assert (NA_WIN_ROWS * GRID_W) % NA_KT_TOKENS == 0
    return pl.pallas_call(
        _natten_kernel,
        grid=(NA_HEADS, n_steps),
        in_specs=[
            pl.BlockSpec((tq, NA_HD), lambda h, rb: (rb, col0 + h)),
            pl.BlockSpec((None,) + knt.shape[1:], lambda h, rb: (h, 0, 0, 0)),
            pl.BlockSpec((s, NA_HD), lambda h, rb: (0, col0 + 2 * NA_HEADS + h)),
            pl.BlockSpec((tq, NA_HD), lambda h, rb: (rb, col0 + 3 * NA_HEADS + h)),
            pl.BlockSpec((1, NA_HD), lambda h, rb: (0, 0)),
            pl.BlockSpec((1,) + tz.shape[1:], lambda h, rb: (h, 0, 0, 0)),
        ],
        out_specs=pl.BlockSpec((tq, NA_HD), lambda h, rb: (rb, h)),
        out_shape=jax.ShapeDtypeStruct((s, NA_W), BF16),
        scratch_shapes=[pltpu.VMEM((3, blk, NA_WIN_ROWS * GRID_W), F32)],
        compiler_params=pltpu.CompilerParams(
            dimension_semantics=("parallel", "arbitrary"),
            vmem_limit_bytes=VMEM_LIMIT),
        name="natten",
    )(u, knt, u, u, qg.reshape(1, NA_HD), tz)


def _merge_kernel(of_ref, ob_ref, hg_ref, bin_ref, ma0_ref, ma1_ref, mb0_ref, mb1_ref,
                  x_ref, p_ref, on_ref, pn_ref, gn_ref, wa_ref, wb_ref, wo_ref, wg_ref, wp_ref,
                  o_ref, *maybe_hn_ref):
    oa = of_ref[...].astype(F32) + ob_ref[...].astype(F32)
    gate_a = _silu(hg_ref[...].astype(F32))
    parts = []
    for h in range(HG_HEADS):
        hs = slice(h * HG_DK, (h + 1) * HG_DK)
        oh = oa[:, hs]
        ms = jnp.mean(oh * oh, axis=-1, keepdims=True)
        parts.append((oh * lax.rsqrt(ms + EPS) * on_ref[...] * gate_a[:, hs]).astype(BF16))
    ya = _dot(jnp.concatenate(parts, axis=1), wa_ref[...])
    yb = _dot(bin_ref[...], wb_ref[...])
    m_a = jnp.concatenate([ma0_ref[...], ma1_ref[...]], axis=1).astype(F32)
    m_b = jnp.concatenate([mb0_ref[...], mb1_ref[...]], axis=1).astype(F32)
    y = _sigmoid(m_a) * ya + _sigmoid(m_b) * yb
    x1 = x_ref[...] + _dot(y.astype(BF16), wo_ref[...])
    g = _sigmoid(_dot(_rms_bf16(x1, pn_ref[...]), wg_ref[...]))
    x2 = x1 + _dot(p_ref[...].astype(BF16), wp_ref[...]) * g
    o_ref[...] = x2
    for hn_ref in maybe_hn_ref:
        hn_ref[...] = _rms_bf16(x2, gn_ref[...])


def _merge(o_f, o_b, u, b_in, x, p, p_index, onorm, pnorm, next_norm, weights, cb_hgate, cb_ma, tm=256):
    s, d = x.shape
    tm = min(tm, s)
    pd = p.shape[-1]
    act = lambda wdt: pl.BlockSpec((tm, wdt), lambda i: (i, 0))
    ucol = lambda cb: pl.BlockSpec((tm, HG_W), lambda i, cb=cb: (i, cb))
    const = lambda shp: pl.BlockSpec(shp, lambda i: (0, 0), pipeline_mode=pl.Buffered(1))
    emit_next = next_norm is not None
    gn = next_norm if emit_next else pnorm
    out_specs = [act(d)] + ([act(d)] if emit_next else [])
    out_shape = [jax.ShapeDtypeStruct((s, d), F32)] + (
        [jax.ShapeDtypeStruct((s, d), BF16)] if emit_next else [])
    res = pl.pallas_call(
        _merge_kernel,
        grid=(s // tm,),
        in_specs=[act(HG_W), act(HG_W), ucol(cb_hgate), act(NA_W),
                  ucol(cb_ma), ucol(cb_ma + 1), ucol(cb_ma + 2), ucol(cb_ma + 3),
                  act(d), pl.BlockSpec((None, tm, pd), lambda i: (p_index, i, 0)),
                  const((1, HG_DK)), const((1, d)), const((1, d))]
                 + [const(w.shape) for w in weights],
        out_specs=out_specs,
        out_shape=out_shape,
        compiler_params=pltpu.CompilerParams(
            dimension_semantics=("parallel",),
            vmem_limit_bytes=VMEM_LIMIT),
        name="merge",
    )(o_f, o_b, u, b_in, u, u, u, u, x, p, onorm.reshape(1, HG_DK), pnorm.reshape(1, d),
      gn.reshape(1, d), *weights)
    return (res[0], res[1]) if emit_next else (res[0], None)


def kernel(x, p, norm_g, w_in, hgrn_lb, hgrn_onorm, na_qnorm, na_knorm, na_rpb,
           w_branch_a, w_branch_b, w_out, ple_norm, w_ple_gate, w_ple):
    bsz, s, d = x.shape
    depth = w_in.shape[0]
    lbp = jax.nn.softmax(hgrn_lb.astype(F32), axis=0)
    lower = jnp.cumsum(lbp, axis=0) - lbp[0:1]
    p_flat = p.reshape(depth * bsz, s, p.shape[-1])
    cb_hgate = 4 * HG_W // HG_W
    cb_na = 5 * HG_W // NA_HD
    cb_ma = (5 * HG_W + 4 * NA_W) // HG_W

    outs = []
    for bi in range(bsz):
        xb = x[bi]
        h = _rmsnorm(xb, norm_g[0])
        for i in range(depth):
            u = _in_proj(h, w_in, i)
            o_f, o_b, knt, weights = _hgrn(u, lower[i], na_knorm[i], (5 * HG_W + NA_W) // NA_W,
                                           (w_branch_a, w_branch_b, w_out, w_ple_gate), i)
            b_in = _natten(u, knt, na_qnorm[i], _natten_toeplitz(na_rpb[i]), col0=cb_na)
            xb, h = _merge(o_f, o_b, u, b_in, xb, p_flat, i * bsz + bi, hgrn_onorm[i], ple_norm[i],
                           norm_g[i + 1] if i + 1 < depth else None,
                           weights + [w_ple[i].astype(BF16)], cb_hgate=cb_hgate, cb_ma=cb_ma)
        outs.append(xb)
    return jnp.stack(outs, axis=0)
```

```python
import functools

import jax
import jax.numpy as jnp
import numpy as np
from jax import lax
from jax.experimental import pallas as pl
from jax.experimental.pallas import tpu as pltpu

F32 = jnp.float32
BF16 = jnp.bfloat16

GRID_W = 64
HG_HEADS = 8
HG_DK = 128
HG_W = HG_HEADS * HG_DK
NA_HEADS = 8
NA_HD = 128
NA_W = NA_HEADS * NA_HD
NA_KH = 8
NA_KW = 16
EPS = 1e-6
NEG = -1e30
F_MIN = 1e-6

CHUNK = 64
HG_CHUNKS_PER_STEP = 4
SUB = 8
NSUB = CHUNK // SUB
MID = SUB // 2

NA_ROWS_PER_STEP = 4
NA_WIN_ROWS = 12
NA_CHAINS = 16
NA_KT_TOKENS = 256
LOG2E = 1.4426950408889634

VMEM_LIMIT = 56 * 1024 * 1024
BF16_ROWS = 16

RMSNORM_TM = 1024
IN_PROJ_TM = 2048
IN_PROJ_TN = 1024
MERGE_TM = 256


def _dot(a, b):
    return jnp.dot(a, b, preferred_element_type=F32)


def _dot_nt(a, b):
    return lax.dot_general(a, b, (((1,), (1,)), ((), ())), preferred_element_type=F32)


def _sigmoid(x):
    return 1.0 / (1.0 + jnp.exp(-x))


def _silu(x):
    return x * _sigmoid(x)


def _rms_bf16(x, g):
    ms = jnp.mean(x * x, axis=-1, keepdims=True)
    return (x * lax.rsqrt(ms + EPS) * g).astype(BF16)


def _rmsnorm_kernel(x_ref, g_ref, o_ref):
    o_ref[...] = _rms_bf16(x_ref[...], g_ref[...])


def _rmsnorm(x, g, tm=RMSNORM_TM):
    s, d = x.shape
    tm = min(tm, s)
    return pl.pallas_call(
        _rmsnorm_kernel,
        grid=(s // tm,),
        in_specs=[pl.BlockSpec((tm, d), lambda i: (i, 0)),
                  pl.BlockSpec((1, d), lambda i: (0, 0))],
        out_specs=pl.BlockSpec((tm, d), lambda i: (i, 0)),
        out_shape=jax.ShapeDtypeStruct((s, d), BF16),
        compiler_params=pltpu.CompilerParams(
            dimension_semantics=("parallel",),
            vmem_limit_bytes=VMEM_LIMIT),
        name="rmsnorm",
    )(x, g.reshape(1, d))


def _inproj_kernel(h_ref, w_ref, o_ref, wb_ref):
    @pl.when(pl.program_id(1) == 0)
    def _():
        wb_ref[...] = w_ref[...].astype(BF16)

    o_ref[...] = _dot(h_ref[...], wb_ref[...]).astype(o_ref.dtype)


def _in_proj(h, w, layer, tm=IN_PROJ_TM, tn=IN_PROJ_TN):
    s, d = h.shape
    n = w.shape[2]
    tm = min(tm, s)
    return pl.pallas_call(
        _inproj_kernel,
        grid=(n // tn, s // tm),
        in_specs=[
            pl.BlockSpec((tm, d), lambda j, i: (i, 0)),
            pl.BlockSpec((None, d, tn), lambda j, i: (layer, 0, j)),
        ],
        out_specs=pl.BlockSpec((tm, tn), lambda j, i: (i, j)),
        out_shape=jax.ShapeDtypeStruct((s, n), BF16),
        scratch_shapes=[pltpu.VMEM((d, tn), BF16)],
        compiler_params=pltpu.CompilerParams(
            dimension_semantics=("parallel", "arbitrary"),
            vmem_limit_bytes=VMEM_LIMIT),
        name="in_proj",
    )(h, w)


def _hgrn_selectors():
    c = CHUNK
    t = np.arange(c)[:, None]
    r = np.arange(c)[None, :]
    m = (t // SUB) * SUB + MID
    out = []
    for backward in (False, True):
        if backward:
            cum = lambda p: (r >= p).astype(np.float32)
            prev_m = m - SUB
        else:
            cum = lambda p: (r <= p).astype(np.float32)
            prev_m = m + SUB
        has_prev = (prev_m >= 0) & (prev_m < c)
        step = np.where(has_prev, cum(np.clip(prev_m, 0, c - 1)) - cum(m), 0.0)
        total = np.ones((BF16_ROWS, c), np.float32)
        sel = np.concatenate([cum(t) - cum(m), cum(m), 1.0 - cum(m), step, total], axis=0)
        out.append(np.concatenate([sel, sel], axis=1))
    return np.stack(out)


HG_SEL_ROWS = 4 * CHUNK + BF16_ROWS


def _hgrn_prologue(q_ref, v_ref, z_ref, rows, lb, sel, backward):
    c = CHUNK
    w = q_ref.shape[1]
    q = q_ref[rows, :].astype(F32)
    v = v_ref[rows, :]
    z = z_ref[rows, :].astype(F32)

    f = lb + (1.0 - lb) * _sigmoid(z)
    g = jnp.log2(jnp.maximum(f, F_MIN))
    k = 1.0 - f
    qs = q * _sigmoid(q)

    hi = g.astype(BF16)
    lo = (g - hi.astype(F32)).astype(BF16)
    cums = _dot(sel, jnp.concatenate([hi, lo], axis=0))
    e = jnp.exp2(cums[0:c])
    to_ref = jnp.exp2(cums[c:2 * c])
    from_ref = jnp.exp2(cums[2 * c:3 * c])
    step = jnp.exp2(cums[3 * c:4 * c])
    dec = jnp.exp2(cums[4 * c:4 * c + SUB])

    q0 = qs * e
    k0 = k * (1.0 / e)
    q_hat = (q0 * to_ref).astype(BF16)
    k_hat = (k0 * from_ref).astype(BF16)

    blocks = [slice(j * SUB, (j + 1) * SUB) for j in range(NSUB)]
    zero_blk = jnp.zeros((SUB, w), F32)
    order = range(NSUB) if backward else range(NSUB - 1, -1, -1)
    cur = [None] * NSUB
    q_tilde = [None] * NSUB
    k_tilde = [None] * NSUB
    for j in order:
        for jj in range(NSUB):
            if cur[jj] is not None:
                cur[jj] = cur[jj] * step[blocks[j]]
        cur[j] = q0[blocks[j]]
        q_tilde[j] = jnp.concatenate(
            [zero_blk if cur[jj] is None else cur[jj] for jj in range(NSUB)], axis=0).astype(BF16)
        k_tilde[j] = jnp.concatenate(
            [k0[blocks[jj]] if jj == j else zero_blk for jj in range(NSUB)], axis=0).astype(BF16)

    row = lax.broadcasted_iota(jnp.int32, (c, c), 0)
    col = lax.broadcasted_iota(jnp.int32, (c, c), 1)
    causal = (row >= col) if backward else (row <= col)
    heads = []
    for h in range(w // HG_DK):
        hs = slice(h * HG_DK, (h + 1) * HG_DK)
        heads.append(dict(
            q_hat=q_hat[:, hs], k_hat=k_hat[:, hs], v=v[:, hs], dec=dec[:, hs],
            q_cat=jnp.concatenate([q_tilde[j][:, hs] for j in range(NSUB)], axis=1),
            k_cat=jnp.concatenate([k_tilde[j][:, hs] for j in range(NSUB)], axis=1)))
    return heads, causal


def _hgrn_scores_t(q_cat, k_cat):
    rt = 2 * SUB
    kw = 2 * HG_DK
    return jnp.concatenate(
        [_dot_nt(k_cat[p * rt:(p + 1) * rt, p * kw:(p + 1) * kw], q_cat[:, p * kw:(p + 1) * kw])
         for p in range(CHUNK // rt)], axis=0)


def _hgrn_kernel(*refs, n_cast):
    (qf_ref, vf_ref, zf_ref, qb_ref, vb_ref, zb_ref, lb_ref, sel_ref, nk_ref, kg_ref) = refs[:10]
    refs = refs[10:]
    w32_refs, refs = refs[:n_cast], refs[n_cast:]
    (of_ref, ob_ref, knt_ref), refs = refs[:3], refs[3:]
    w16_refs, (stf_ref, stb_ref) = refs[:n_cast], refs[n_cast:]

    @pl.when(pl.program_id(0) == 0)
    def _():
        stf_ref[...] = jnp.zeros_like(stf_ref)
        stb_ref[...] = jnp.zeros_like(stb_ref)

    for w32_ref, w16_ref in zip(w32_refs, w16_refs):
        w16_ref[...] = w32_ref[...].astype(w16_ref.dtype)
    for h in range(knt_ref.shape[0]):
        kk = nk_ref[:, h * NA_HD:(h + 1) * NA_HD].astype(F32)
        knt_ref[h, 0] = _rms_bf16(kk, kg_ref[...]).T

    n_heads = stf_ref.shape[0]
    n_chunks = qf_ref.shape[0] // CHUNK
    rows_f = [slice(c * CHUNK, (c + 1) * CHUNK) for c in range(n_chunks)]
    rows_b = rows_f[::-1]
    pro = [(_hgrn_prologue(qf_ref, vf_ref, zf_ref, rf, lb_ref[0:1, :], sel_ref[0], False),
            _hgrn_prologue(qb_ref, vb_ref, zb_ref, rb, lb_ref[1:2, :], sel_ref[1], True))
           for rf, rb in zip(rows_f, rows_b)]

    states = [stf_ref[h] for h in range(n_heads)] + [stb_ref[h] for h in range(n_heads)]
    for (rf, rb), ((heads_f, causal_f), (heads_b, causal_b)) in zip(zip(rows_f, rows_b), pro):
        work = ([(heads_f[h], causal_f, of_ref, rf, h) for h in range(n_heads)]
                + [(heads_b[h], causal_b, ob_ref, rb, h) for h in range(n_heads)])
        atts = [_hgrn_scores_t(a["q_cat"], a["k_cat"]) for (a, _, _, _, _) in work]
        updates = [_dot(a["v"].T, a["k_hat"]) for (a, _, _, _, _) in work]
        atts = [jnp.where(causal, att, 0.0).astype(BF16).T
                for att, (_, causal, _, _, _) in zip(atts, work)]
        outs = [_dot(jnp.concatenate([a["q_hat"], att], axis=1),
                     jnp.concatenate([st.astype(BF16).T, a["v"]], axis=0))
                for att, st, (a, _, _, _, _) in zip(atts, states, work)]
        new_states = []
        for (a, _, o_ref, rows, h), st, out, upd in zip(work, states, outs, updates):
            o_ref[rows, h * HG_DK:(h + 1) * HG_DK] = out.astype(o_ref.dtype)
            decayed = st.reshape(HG_DK // SUB, SUB, HG_DK) * a["dec"][None]
            new_states.append(decayed.reshape(HG_DK, HG_DK) + upd)
        states = new_states
    for h in range(n_heads):
        stf_ref[h] = states[h]
        stb_ref[h] = states[n_heads + h]


def _hgrn(u, lb, kg, cb_nk, cast_weights=(), layer=0):
    s = u.shape[0]
    t = HG_CHUNKS_PER_STEP * CHUNK
    n = s // t
    assert s % t == 0 and NA_KT_TOKENS % t == 0 and t % 128 == 0
    per_tile = NA_KT_TOKENS // t
    fwd = lambda cb: pl.BlockSpec((t, HG_W), lambda i, cb=cb: (i, cb))
    bwd = lambda cb: pl.BlockSpec((t, HG_W), lambda i, cb=cb: (n - 1 - i, cb))
    heads = HG_W // HG_DK
    cast_rows = [w.shape[1] // n for w in cast_weights]
    assert all(w.shape[1] == r * n and r % BF16_ROWS == 0 for w, r in zip(cast_weights, cast_rows))
    res = pl.pallas_call(
        functools.partial(_hgrn_kernel, n_cast=len(cast_weights)),
        grid=(n,),
        in_specs=[fwd(0), fwd(1), fwd(2), bwd(0), bwd(1), bwd(3),
                  pl.BlockSpec((2, HG_W), lambda i: (0, 0)),
                  pl.BlockSpec((2, HG_SEL_ROWS, 2 * CHUNK), lambda i: (0, 0, 0)),
                  pl.BlockSpec((t, NA_W), lambda i: (i, cb_nk)),
                  pl.BlockSpec((1, NA_HD), lambda i: (0, 0))]
                 + [pl.BlockSpec((None, r, w.shape[2]), lambda i: (layer, i, 0))
                    for w, r in zip(cast_weights, cast_rows)],
        out_specs=[pl.BlockSpec((t, HG_W), lambda i: (i, 0)),
                   pl.BlockSpec((t, HG_W), lambda i: (n - 1 - i, 0)),
                   pl.BlockSpec((NA_HEADS, 1, NA_HD, t), lambda i: (0, i // per_tile, 0, i % per_tile))]
                  + [pl.BlockSpec((r, w.shape[2]), lambda i: (i, 0))
                     for w, r in zip(cast_weights, cast_rows)],
        out_shape=[jax.ShapeDtypeStruct((s, HG_W), BF16)] * 2
                  + [jax.ShapeDtypeStruct((NA_HEADS, s // NA_KT_TOKENS, NA_HD, NA_KT_TOKENS), BF16)]
                  + [jax.ShapeDtypeStruct(w.shape[1:], BF16) for w in cast_weights],
        scratch_shapes=[pltpu.VMEM((heads, HG_DK, HG_DK), F32)] * 2,
        compiler_params=pltpu.CompilerParams(
            dimension_semantics=("arbitrary",),
            vmem_limit_bytes=VMEM_LIMIT),
        name="hgrn",
    )(u, u, u, u, u, u, lb, jnp.asarray(_hgrn_selectors(), BF16), u, kg.reshape(1, NA_HD),
      *cast_weights)
    return res[0], res[1], res[2], list(res[3:])


NA_NEG_BLOCK = 2 * NA_KH - 1


def _natten_fill_bias(tz_ref, bias_ref, w0, r0, rows):
    kh = min(NA_KH, rows)
    lane = lax.broadcasted_iota(jnp.int32, (GRID_W, 2 * GRID_W), 1)

    def block_index(qi, a):
        r = r0 + qi
        r_start = min(max(r - kh // 2, 0), rows - kh)
        ar = w0 + a
        if r_start <= ar < r_start + kh:
            return ar - r + (NA_KH - 1)
        return NA_NEG_BLOCK

    for qi in range(NA_ROWS_PER_STEP):
        for ap in range(NA_WIN_ROWS // 2):
            ie, io = block_index(qi, 2 * ap), block_index(qi, 2 * ap + 1)
            blk = tz_ref[0, ie]
            if io != ie:
                blk = jnp.where(lane < GRID_W, blk, tz_ref[0, io])
            bias_ref[qi * GRID_W:(qi + 1) * GRID_W, ap * 2 * GRID_W:(ap + 1) * 2 * GRID_W] = blk


def _natten_kernel(q_ref, knt_ref, v_ref, gate_ref, qg_ref, tz_ref, o_ref, bias_ref):
    step = pl.program_id(1)
    s = v_ref.shape[0]
    rows = s // GRID_W
    rq, rk = NA_ROWS_PER_STEP, NA_WIN_ROWS
    n_blocks = rows // rq
    tq, win = rq * GRID_W, rk * GRID_W
    win_tiles = win // NA_KT_TOKENS

    @pl.when(step == 0)
    def _():
        _natten_fill_bias(tz_ref, bias_ref.at[0], 0, 0, rows)
        _natten_fill_bias(tz_ref, bias_ref.at[1], max(rq - NA_KH // 2, 0), rq, rows)
        _natten_fill_bias(tz_ref, bias_ref.at[2], rows - rk, rows - rq, rows)

    q = q_ref[...].astype(F32)
    ms = jnp.mean(q * q, axis=-1, keepdims=True)
    qn = (q * lax.rsqrt(ms + EPS) * (qg_ref[...] * (NA_HD ** -0.5 * LOG2E))).astype(BF16)

    qs, kws, vws, variants = [], [], [], []
    for c in range(NA_CHAINS):
        rb = step * NA_CHAINS + c
        w0 = jnp.clip(rq * rb - NA_KH // 2, 0, rows - rk) * GRID_W
        w0 = pl.multiple_of(w0, GRID_W)
        qs.append(qn[c * tq:(c + 1) * tq])
        kt = knt_ref[pl.ds(w0 // NA_KT_TOKENS, win_tiles)]
        kws.append(jnp.concatenate([kt[i] for i in range(win_tiles)], axis=1))
        vws.append(v_ref[pl.ds(w0, win), :])
        variants.append(jnp.where(rb == 0, 0, jnp.where(rb == n_blocks - 1, 2, 1)))
    scs = [_dot(qc, kw) for qc, kw in zip(qs, kws)]
    scs = [sc + bias_ref[var] for sc, var in zip(scs, variants)]
    ms = [jnp.max(sc, axis=-1, keepdims=True) for sc in scs]
    ps = [jnp.exp2(sc - m) for sc, m in zip(scs, ms)]
    ls = [jnp.sum(p, axis=-1, keepdims=True) for p in ps]
    outs = [_dot(p.astype(BF16), vw) for p, vw in zip(ps, vws)]
    for c, (o, l) in enumerate(zip(outs, ls)):
        gate = _silu(gate_ref[c * tq:(c + 1) * tq, :].astype(F32))
        o_ref[c * tq:(c + 1) * tq, :] = (o / l * gate).astype(o_ref.dtype)


def _natten_toeplitz(rpb):
    c = np.arange(GRID_W)
    c_start = np.clip(c - NA_KW // 2, 0, GRID_W - NA_KW)
    col_in = (c[None, :] >= c_start[:, None]) & (c[None, :] < c_start[:, None] + NA_KW)
    col_off = np.clip(c[None, :] - c[:, None] + (NA_KW - 1), 0, 2 * NA_KW - 2)
    sel = ((col_off[..., None] == np.arange(2 * NA_KW - 1)) & col_in[..., None]).astype(np.float32)
    tz = jnp.einsum('hij,ckj->hick', rpb.astype(F32), sel, precision=lax.Precision.HIGHEST)
    tz = jnp.where(col_in[None, None], tz * LOG2E, NEG)
    tz = jnp.concatenate([tz, jnp.full_like(tz[:, :1], NEG)], axis=1)
    return jnp.concatenate([tz, tz], axis=-1)


def _natten(u, knt, qg, tz, col0):
    s = u.shape[0]
    blk = NA_ROWS_PER_STEP * GRID_W
    tq = NA_CHAINS * blk
    n_steps = s // tq
    assert s % tq == 0 and s // blk >= 3 and s // GRID_W >= NA_WIN_ROWS
    assert blk % NA_KT_TOKENS == 0 and (NA_KH // 2 * GRID_W) % NA_KT_TOKENS == 0
    assert (NA_WIN_ROWS * GRID_W) % NA_KT_TOKENS == 0
    return pl.pallas_call(
        _natten_kernel,
        grid=(NA_HEADS, n_steps),
        in_specs=[
            pl.BlockSpec((tq, NA_HD), lambda h, rb: (rb, col0 + h)),
            pl.BlockSpec((None,) + knt.shape[1:], lambda h, rb: (h, 0, 0, 0)),
            pl.BlockSpec((s, NA_HD), lambda h, rb: (0, col0 + 2 * NA_HEADS + h)),
            pl.BlockSpec((tq, NA_HD), lambda h, rb: (rb, col0 + 3 * NA_HEADS + h)),
            pl.BlockSpec((1, NA_HD), lambda h, rb: (0, 0)),
            pl.BlockSpec((1,) + tz.shape[1:], lambda h, rb: (h, 0, 0, 0)),
        ],
        out_specs=pl.BlockSpec((tq, NA_HD), lambda h, rb: (rb, h)),
        out_shape=jax.ShapeDtypeStruct((s, NA_W), BF16),
        scratch_shapes=[pltpu.VMEM((3, blk, NA_WIN_ROWS * GRID_W), F32)],
        compiler_params=pltpu.CompilerParams(
            dimension_semantics=("parallel", "arbitrary"),
            vmem_limit_bytes=VMEM_LIMIT),
        name="natten",
    )(u, knt, u, u, qg.reshape(1, NA_HD), tz)


def _merge_kernel(of_ref, ob_ref, hg_ref, bin_ref, ma0_ref, ma1_ref, mb0_ref, mb1_ref,
                  x_ref, p_ref, on_ref, pn_ref, gn_ref, wa_ref, wb_ref, wo_ref, wg_ref, wp_ref,
                  o_ref, *maybe_hn_ref):
    oa = of_ref[...].astype(F32) + ob_ref[...].astype(F32)
    gate_a = _silu(hg_ref[...].astype(F32))
    parts = []
    for h in range(HG_HEADS):
        hs = slice(h * HG_DK, (h + 1) * HG_DK)
        oh = oa[:, hs]
        ms = jnp.mean(oh * oh, axis=-1, keepdims=True)
        parts.append((oh * lax.rsqrt(ms + EPS) * on_ref[...] * gate_a[:, hs]).astype(BF16))
    ya = _dot(jnp.concatenate(parts, axis=1), wa_ref[...])
    yb = _dot(bin_ref[...], wb_ref[...])
    m_a = jnp.concatenate([ma0_ref[...], ma1_ref[...]], axis=1).astype(F32)
    m_b = jnp.concatenate([mb0_ref[...], mb1_ref[...]], axis=1).astype(F32)
    y = _sigmoid(m_a) * ya + _sigmoid(m_b) * yb
    x1 = x_ref[...] + _dot(y.astype(BF16), wo_ref[...])
    inv = lax.rsqrt(jnp.mean(x1 * x1, axis=-1, keepdims=True) + EPS)
    g = _sigmoid(_dot((x1 * pn_ref[...]).astype(BF16), wg_ref[...]) * inv)
    x2 = x1 + _dot(p_ref[...].astype(BF16), wp_ref[...]) * g
    o_ref[...] = x2
    for hn_ref in maybe_hn_ref:
        hn_ref[...] = _rms_bf16(x2, gn_ref[...])


def _merge(o_f, o_b, u, b_in, x, p, p_index, onorm, pnorm, next_norm, weights, cb_hgate, cb_ma, tm=MERGE_TM):
    s, d = x.shape
    tm = min(tm, s)
    pd = p.shape[-1]
    act = lambda wdt: pl.BlockSpec((tm, wdt), lambda i: (i, 0))
    ucol = lambda cb: pl.BlockSpec((tm, HG_W), lambda i, cb=cb: (i, cb))
    const = lambda shp: pl.BlockSpec(shp, lambda i: (0, 0), pipeline_mode=pl.Buffered(1))
    emit_next = next_norm is not None
    gn = next_norm if emit_next else pnorm
    out_specs = [act(d)] + ([act(d)] if emit_next else [])
    out_shape = [jax.ShapeDtypeStruct((s, d), F32)] + (
        [jax.ShapeDtypeStruct((s, d), BF16)] if emit_next else [])
    res = pl.pallas_call(
        _merge_kernel,
        grid=(s // tm,),
        in_specs=[act(HG_W), act(HG_W), ucol(cb_hgate), act(NA_W),
                  ucol(cb_ma), ucol(cb_ma + 1), ucol(cb_ma + 2), ucol(cb_ma + 3),
                  act(d), pl.BlockSpec((None, tm, pd), lambda i: (p_index, i, 0)),
                  const((1, HG_DK)), const((1, d)), const((1, d))]
                 + [const(w.shape) for w in weights],
        out_specs=out_specs,
        out_shape=out_shape,
        compiler_params=pltpu.CompilerParams(
            dimension_semantics=("parallel",),
            vmem_limit_bytes=VMEM_LIMIT),
        name="merge",
    )(o_f, o_b, u, b_in, u, u, u, u, x, p, onorm.reshape(1, HG_DK), pnorm.reshape(1, d),
      gn.reshape(1, d), *weights)
    return (res[0], res[1]) if emit_next else (res[0], None)


def kernel(x, p, norm_g, w_in, hgrn_lb, hgrn_onorm, na_qnorm, na_knorm, na_rpb,
           w_branch_a, w_branch_b, w_out, ple_norm, w_ple_gate, w_ple):
    bsz, s, d = x.shape
    depth = w_in.shape[0]
    lbp = jax.nn.softmax(hgrn_lb.astype(F32), axis=0)
    lower = jnp.cumsum(lbp, axis=0) - lbp[0:1]
    p_flat = p.reshape(depth * bsz, s, p.shape[-1])
    cb_hgate = 4 * HG_W // HG_W
    cb_na = 5 * HG_W // NA_HD
    cb_ma = (5 * HG_W + 4 * NA_W) // HG_W

    outs = []
    for bi in range(bsz):
        xb = x[bi]
        h = _rmsnorm(xb, norm_g[0])
        for i in range(depth):
            u = _in_proj(h, w_in, i)
            o_f, o_b, knt, weights = _hgrn(u, lower[i], na_knorm[i], (5 * HG_W + NA_W) // NA_W,
                                           (w_branch_a, w_branch_b, w_out, w_ple_gate), i)
            b_in = _natten(u, knt, na_qnorm[i], _natten_toeplitz(na_rpb[i]), col0=cb_na)
            xb, h = _merge(o_f, o_b, u, b_in, xb, p_flat, i * bsz + bi, hgrn_onorm[i], ple_norm[i],
                           norm_g[i + 1] if i + 1 < depth else None,
                           weights + [w_ple[i].astype(BF16)], cb_hgate=cb_hgate, cb_ma=cb_ma)
        outs.append(xb)
    return jnp.stack(outs, axis=0)
```

```python
import functools

import jax
import jax.numpy as jnp
import numpy as np
from jax import lax
from jax.experimental import pallas as pl
from jax.experimental.pallas import tpu as pltpu

F32 = jnp.float32
BF16 = jnp.bfloat16

GRID_W = 64
HG_HEADS = 8
HG_DK = 128
HG_W = HG_HEADS * HG_DK
NA_HEADS = 8
NA_HD = 128
NA_W = NA_HEADS * NA_HD
NA_KH = 8
NA_KW = 16
EPS = 1e-6
NEG = -1e30
F_MIN = 1e-6

CHUNK = 64
HG_CHUNKS_PER_STEP = 2
SUB = 8
NSUB = CHUNK // SUB
MID = SUB // 2

NA_ROWS_PER_STEP = 4
NA_WIN_ROWS = 12
NA_CHAINS = 8
NA_KT_TOKENS = 256
LOG2E = 1.4426950408889634

VMEM_LIMIT = 56 * 1024 * 1024
BF16_ROWS = 16

RMSNORM_TM = 1024
IN_PROJ_TM = 2048
IN_PROJ_TN = 1024
MERGE_TM = 256


def _dot(a, b):
    return jnp.dot(a, b, preferred_element_type=F32)


def _dot_nt(a, b):
    return lax.dot_general(a, b, (((1,), (1,)), ((), ())), preferred_element_type=F32)


def _sigmoid(x):
    return 1.0 / (1.0 + jnp.exp(-x))


def _silu(x):
    return x * _sigmoid(x)


def _rms_bf16(x, g):
    ms = jnp.mean(x * x, axis=-1, keepdims=True)
    return (x * lax.rsqrt(ms + EPS) * g).astype(BF16)


def _rmsnorm_kernel(x_ref, g_ref, o_ref):
    o_ref[...] = _rms_bf16(x_ref[...], g_ref[...])


def _rmsnorm(x, g, tm=RMSNORM_TM):
    s, d = x.shape
    tm = min(tm, s)
    return pl.pallas_call(
        _rmsnorm_kernel,
        grid=(s // tm,),
        in_specs=[pl.BlockSpec((tm, d), lambda i: (i, 0)),
                  pl.BlockSpec((1, d), lambda i: (0, 0))],
        out_specs=pl.BlockSpec((tm, d), lambda i: (i, 0)),
        out_shape=jax.ShapeDtypeStruct((s, d), BF16),
        compiler_params=pltpu.CompilerParams(
            dimension_semantics=("parallel",),
            vmem_limit_bytes=VMEM_LIMIT),
        name="rmsnorm",
    )(x, g.reshape(1, d))


def _inproj_kernel(h_ref, w_ref, o_ref, wb_ref):
    @pl.when(pl.program_id(1) == 0)
    def _():
        wb_ref[...] = w_ref[...].astype(BF16)

    o_ref[...] = _dot(h_ref[...], wb_ref[...]).astype(o_ref.dtype)


def _in_proj(h, w, layer, tm=IN_PROJ_TM, tn=IN_PROJ_TN):
    s, d = h.shape
    n = w.shape[2]
    tm = min(tm, s)
    return pl.pallas_call(
        _inproj_kernel,
        grid=(n // tn, s // tm),
        in_specs=[
            pl.BlockSpec((tm, d), lambda j, i: (i, 0)),
            pl.BlockSpec((None, d, tn), lambda j, i: (layer, 0, j)),
        ],
        out_specs=pl.BlockSpec((tm, tn), lambda j, i: (i, j)),
        out_shape=jax.ShapeDtypeStruct((s, n), BF16),
        scratch_shapes=[pltpu.VMEM((d, tn), BF16)],
        compiler_params=pltpu.CompilerParams(
            dimension_semantics=("parallel", "arbitrary"),
            vmem_limit_bytes=VMEM_LIMIT),
        name="in_proj",
    )(h, w)


def _hgrn_selectors():
    c = CHUNK
    t = np.arange(c)[:, None]
    r = np.arange(c)[None, :]
    m = (t // SUB) * SUB + MID
    out = []
    for backward in (False, True):
        if backward:
            cum = lambda p: (r >= p).astype(np.float32)
            prev_m = m - SUB
        else:
            cum = lambda p: (r <= p).astype(np.float32)
            prev_m = m + SUB
        has_prev = (prev_m >= 0) & (prev_m < c)
        step = np.where(has_prev, cum(np.clip(prev_m, 0, c - 1)) - cum(m), 0.0)
        total = np.ones((BF16_ROWS, c), np.float32)
        sel = np.concatenate([cum(t) - cum(m), cum(m), 1.0 - cum(m), step, total], axis=0)
        out.append(np.concatenate([sel, sel], axis=1))
    return np.stack(out)


HG_SEL_ROWS = 4 * CHUNK + BF16_ROWS


def _hgrn_prologue(q_ref, v_ref, z_ref, rows, lb, sel, backward):
    c = CHUNK
    w = q_ref.shape[1]
    q = q_ref[rows, :].astype(F32)
    v = v_ref[rows, :]
    z = z_ref[rows, :].astype(F32)

    f = lb + (1.0 - lb) * _sigmoid(z)
    g = jnp.log2(jnp.maximum(f, F_MIN))
    k = 1.0 - f
    qs = q * _sigmoid(q)

    hi = g.astype(BF16)
    lo = (g - hi.astype(F32)).astype(BF16)
    cums = _dot(sel, jnp.concatenate([hi, lo], axis=0))
    e = jnp.exp2(cums[0:c])
    to_ref = jnp.exp2(cums[c:2 * c])
    from_ref = jnp.exp2(cums[2 * c:3 * c])
    step = jnp.exp2(cums[3 * c:4 * c])
    dec = jnp.exp2(cums[4 * c:4 * c + SUB])

    q0 = qs * e
    k0 = k * (1.0 / e)
    q_hat = (q0 * to_ref).astype(BF16)
    k_hat = (k0 * from_ref).astype(BF16)

    blocks = [slice(j * SUB, (j + 1) * SUB) for j in range(NSUB)]
    zero_blk = jnp.zeros((SUB, w), F32)
    order = range(NSUB) if backward else range(NSUB - 1, -1, -1)
    cur = [None] * NSUB
    q_tilde = [None] * NSUB
    k_tilde = [None] * NSUB
    for j in order:
        for jj in range(NSUB):
            if cur[jj] is not None:
                cur[jj] = cur[jj] * step[blocks[j]]
        cur[j] = q0[blocks[j]]
        q_tilde[j] = jnp.concatenate(
            [zero_blk if cur[jj] is None else cur[jj] for jj in range(NSUB)], axis=0).astype(BF16)
        k_tilde[j] = jnp.concatenate(
            [k0[blocks[jj]] if jj == j else zero_blk for jj in range(NSUB)], axis=0).astype(BF16)

    row = lax.broadcasted_iota(jnp.int32, (c, c), 0)
    col = lax.broadcasted_iota(jnp.int32, (c, c), 1)
    causal = (row >= col) if backward else (row <= col)
    heads = []
    for h in range(w // HG_DK):
        hs = slice(h * HG_DK, (h + 1) * HG_DK)
        heads.append(dict(
            q_hat=q_hat[:, hs], k_hat=k_hat[:, hs], v=v[:, hs], dec=dec[:, hs],
            q_cat=jnp.concatenate([q_tilde[j][:, hs] for j in range(NSUB)], axis=1),
            k_cat=jnp.concatenate([k_tilde[j][:, hs] for j in range(NSUB)], axis=1)))
    return heads, causal


def _hgrn_scores_t(q_cat, k_cat):
    rt = 2 * SUB
    kw = 2 * HG_DK
    return jnp.concatenate(
        [_dot_nt(k_cat[p * rt:(p + 1) * rt, p * kw:(p + 1) * kw], q_cat[:, p * kw:(p + 1) * kw])
         for p in range(CHUNK // rt)], axis=0)


def _hgrn_kernel(*refs, n_cast):
    (qf_ref, vf_ref, zf_ref, qb_ref, vb_ref, zb_ref, lb_ref, sel_ref, nk_ref, kg_ref) = refs[:10]
    refs = refs[10:]
    w32_refs, refs = refs[:n_cast], refs[n_cast:]
    (of_ref, ob_ref, knt_ref), refs = refs[:3], refs[3:]
    w16_refs, (stf_ref, stb_ref) = refs[:n_cast], refs[n_cast:]

    @pl.when(pl.program_id(0) == 0)
    def _():
        stf_ref[...] = jnp.zeros_like(stf_ref)
        stb_ref[...] = jnp.zeros_like(stb_ref)

    for w32_ref, w16_ref in zip(w32_refs, w16_refs):
        w16_ref[...] = w32_ref[...].astype(w16_ref.dtype)
    for h in range(knt_ref.shape[0]):
        kk = nk_ref[:, h * NA_HD:(h + 1) * NA_HD].astype(F32)
        knt_ref[h, 0] = _rms_bf16(kk, kg_ref[...]).T

    n_heads = stf_ref.shape[0]
    n_chunks = qf_ref.shape[0] // CHUNK
    rows_f = [slice(c * CHUNK, (c + 1) * CHUNK) for c in range(n_chunks)]
    rows_b = rows_f[::-1]
    pro = [(_hgrn_prologue(qf_ref, vf_ref, zf_ref, rf, lb_ref[0:1, :], sel_ref[0], False),
            _hgrn_prologue(qb_ref, vb_ref, zb_ref, rb, lb_ref[1:2, :], sel_ref[1], True))
           for rf, rb in zip(rows_f, rows_b)]

    states = [stf_ref[h] for h in range(n_heads)] + [stb_ref[h] for h in range(n_heads)]
    for (rf, rb), ((heads_f, causal_f), (heads_b, causal_b)) in zip(zip(rows_f, rows_b), pro):
        work = ([(heads_f[h], causal_f, of_ref, rf, h) for h in range(n_heads)]
                + [(heads_b[h], causal_b, ob_ref, rb, h) for h in range(n_heads)])
        atts = [_hgrn_scores_t(a["q_cat"], a["k_cat"]) for (a, _, _, _, _) in work]
        updates = [_dot(a["v"].T, a["k_hat"]) for (a, _, _, _, _) in work]
        atts = [jnp.where(causal, att, 0.0).astype(BF16).T
                for att, (_, causal, _, _, _) in zip(atts, work)]
        outs = [_dot(jnp.concatenate([a["q_hat"], att], axis=1),
                     jnp.concatenate([st.astype(BF16).T, a["v"]], axis=0))
                for att, st, (a, _, _, _, _) in zip(atts, states, work)]
        new_states = []
        for (a, _, o_ref, rows, h), st, out, upd in zip(work, states, outs, updates):
            o_ref[rows, h * HG_DK:(h + 1) * HG_DK] = out.astype(o_ref.dtype)
            decayed = st.reshape(HG_DK // SUB, SUB, HG_DK) * a["dec"][None]
            new_states.append(decayed.reshape(HG_DK, HG_DK) + upd)
        states = new_states
    for h in range(n_heads):
        stf_ref[h] = states[h]
        stb_ref[h] = states[n_heads + h]


def _hgrn(u, lb, kg, cb_nk, cast_weights=(), layer=0):
    s = u.shape[0]
    t = HG_CHUNKS_PER_STEP * CHUNK
    n = s // t
    assert s % t == 0 and NA_KT_TOKENS % t == 0 and t % 128 == 0
    per_tile = NA_KT_TOKENS // t
    fwd = lambda cb: pl.BlockSpec((t, HG_W), lambda i, cb=cb: (i, cb))
    bwd = lambda cb: pl.BlockSpec((t, HG_W), lambda i, cb=cb: (n - 1 - i, cb))
    heads = HG_W // HG_DK
    cast_rows = [w.shape[1] // n for w in cast_weights]
    assert all(w.shape[1] == r * n and r % BF16_ROWS == 0 for w, r in zip(cast_weights, cast_rows))
    res = pl.pallas_call(
        functools.partial(_hgrn_kernel, n_cast=len(cast_weights)),
        grid=(n,),
        in_specs=[fwd(0), fwd(1), fwd(2), bwd(0), bwd(1), bwd(3),
                  pl.BlockSpec((2, HG_W), lambda i: (0, 0)),
                  pl.BlockSpec((2, HG_SEL_ROWS, 2 * CHUNK), lambda i: (0, 0, 0)),
                  pl.BlockSpec((t, NA_W), lambda i: (i, cb_nk)),
                  pl.BlockSpec((1, NA_HD), lambda i: (0, 0))]
                 + [pl.BlockSpec((None, r, w.shape[2]), lambda i: (layer, i, 0))
                    for w, r in zip(cast_weights, cast_rows)],
        out_specs=[pl.BlockSpec((t, HG_W), lambda i: (i, 0)),
                   pl.BlockSpec((t, HG_W), lambda i: (n - 1 - i, 0)),
                   pl.BlockSpec((NA_HEADS, 1, NA_HD, t), lambda i: (0, i // per_tile, 0, i % per_tile))]
                  + [pl.BlockSpec((r, w.shape[2]), lambda i: (i, 0))
                     for w, r in zip(cast_weights, cast_rows)],
        out_shape=[jax.ShapeDtypeStruct((s, HG_W), BF16)] * 2
                  + [jax.ShapeDtypeStruct((NA_HEADS, s // NA_KT_TOKENS, NA_HD, NA_KT_TOKENS), BF16)]
                  + [jax.ShapeDtypeStruct(w.shape[1:], BF16) for w in cast_weights],
        scratch_shapes=[pltpu.VMEM((heads, HG_DK, HG_DK), F32)] * 2,
        compiler_params=pltpu.CompilerParams(
            dimension_semantics=("arbitrary",),
            vmem_limit_bytes=VMEM_LIMIT),
        name="hgrn",
    )(u, u, u, u, u, u, lb, jnp.asarray(_hgrn_selectors(), BF16), u, kg.reshape(1, NA_HD),
      *cast_weights)
    return res[0], res[1], res[2], list(res[3:])


NA_NEG_BLOCK = 2 * NA_KH - 1


def _natten_fill_bias(tz_ref, bias_ref, w0, r0, rows):
    kh = min(NA_KH, rows)
    lane = lax.broadcasted_iota(jnp.int32, (GRID_W, 2 * GRID_W), 1)

    def block_index(qi, a):
        r = r0 + qi
        r_start = min(max(r - kh // 2, 0), rows - kh)
        ar = w0 + a
        if r_start <= ar < r_start + kh:
            return ar - r + (NA_KH - 1)
        return NA_NEG_BLOCK

    for qi in range(NA_ROWS_PER_STEP):
        for ap in range(NA_WIN_ROWS // 2):
            ie, io = block_index(qi, 2 * ap), block_index(qi, 2 * ap + 1)
            blk = tz_ref[0, ie]
            if io != ie:
                blk = jnp.where(lane < GRID_W, blk, tz_ref[0, io])
            bias_ref[qi * GRID_W:(qi + 1) * GRID_W, ap * 2 * GRID_W:(ap + 1) * 2 * GRID_W] = blk


def _natten_kernel(q_ref, knt_ref, v_ref, gate_ref, qg_ref, tz_ref, o_ref, bias_ref):
    step = pl.program_id(1)
    s = v_ref.shape[0]
    rows = s // GRID_W
    rq, rk = NA_ROWS_PER_STEP, NA_WIN_ROWS
    n_blocks = rows // rq
    tq, win = rq * GRID_W, rk * GRID_W
    win_tiles = win // NA_KT_TOKENS

    @pl.when(step == 0)
    def _():
        _natten_fill_bias(tz_ref, bias_ref.at[0], 0, 0, rows)
        _natten_fill_bias(tz_ref, bias_ref.at[1], max(rq - NA_KH // 2, 0), rq, rows)
        _natten_fill_bias(tz_ref, bias_ref.at[2], rows - rk, rows - rq, rows)

    q = q_ref[...].astype(F32)
    ms = jnp.mean(q * q, axis=-1, keepdims=True)
    qn = (q * lax.rsqrt(ms + EPS) * (qg_ref[...] * (NA_HD ** -0.5 * LOG2E))).astype(BF16)

    qs, kws, vws, variants = [], [], [], []
    for c in range(NA_CHAINS):
        rb = step * NA_CHAINS + c
        w0 = jnp.clip(rq * rb - NA_KH // 2, 0, rows - rk) * GRID_W
        w0 = pl.multiple_of(w0, GRID_W)
        qs.append(qn[c * tq:(c + 1) * tq])
        kt = knt_ref[pl.ds(w0 // NA_KT_TOKENS, win_tiles)]
        kws.append(jnp.concatenate([kt[i] for i in range(win_tiles)], axis=1))
        vws.append(v_ref[pl.ds(w0, win), :])
        variants.append(jnp.where(rb == 0, 0, jnp.where(rb == n_blocks - 1, 2, 1)))
    scs = [_dot(qc, kw) for qc, kw in zip(qs, kws)]
    scs = [sc + bias_ref[var] for sc, var in zip(scs, variants)]
    ms = [jnp.max(sc, axis=-1, keepdims=True) for sc in scs]
    ps = [jnp.exp2(sc - m) for sc, m in zip(scs, ms)]
    ls = [jnp.sum(p, axis=-1, keepdims=True) for p in ps]
    outs = [_dot(p.astype(BF16), vw) for p, vw in zip(ps, vws)]
    for c, (o, l) in enumerate(zip(outs, ls)):
        gate = _silu(gate_ref[c * tq:(c + 1) * tq, :].astype(F32))
        o_ref[c * tq:(c + 1) * tq, :] = (o / l * gate).astype(o_ref.dtype)


def _natten_toeplitz(rpb):
    c = np.arange(GRID_W)
    c_start = np.clip(c - NA_KW // 2, 0, GRID_W - NA_KW)
    col_in = (c[None, :] >= c_start[:, None]) & (c[None, :] < c_start[:, None] + NA_KW)
    col_off = np.clip(c[None, :] - c[:, None] + (NA_KW - 1), 0, 2 * NA_KW - 2)
    sel = ((col_off[..., None] == np.arange(2 * NA_KW - 1)) & col_in[..., None]).astype(np.float32)
    tz = jnp.einsum('hij,ckj->hick', rpb.astype(F32), sel, precision=lax.Precision.HIGHEST)
    tz = jnp.where(col_in[None, None], tz * LOG2E, NEG)
    tz = jnp.concatenate([tz, jnp.full_like(tz[:, :1], NEG)], axis=1)
    return jnp.concatenate([tz, tz], axis=-1)


def _natten(u, knt, qg, tz, col0):
    s = u.shape[0]
    blk = NA_ROWS_PER_STEP * GRID_W
    tq = NA_CHAINS * blk
    n_steps = s // tq
    assert s % tq == 0 and s // blk >= 3 and s // GRID_W >= NA_WIN_ROWS
    assert blk % NA_KT_TOKENS == 0 and (NA_KH // 2 * GRID_W) % NA_KT_TOKENS == 0
    assert (NA_WIN_ROWS * GRID_W) % NA_KT_TOKENS == 0
    return pl.pallas_call(
        _natten_kernel,
        grid=(NA_HEADS, n_steps),
        in_specs=[
            pl.BlockSpec((tq, NA_HD), lambda h, rb: (rb, col0 + h)),
            pl.BlockSpec((None,) + knt.shape[1:], lambda h, rb: (h, 0, 0, 0)),
            pl.BlockSpec((s, NA_HD), lambda h, rb: (0, col0 + 2 * NA_HEADS + h)),
            pl.BlockSpec((tq, NA_HD), lambda h, rb: (rb, col0 + 3 * NA_HEADS + h)),
            pl.BlockSpec((1, NA_HD), lambda h, rb: (0, 0)),
            pl.BlockSpec((1,) + tz.shape[1:], lambda h, rb: (h, 0, 0, 0)),
        ],
        out_specs=pl.BlockSpec((tq, NA_HD), lambda h, rb: (rb, h)),
        out_shape=jax.ShapeDtypeStruct((s, NA_W), BF16),
        scratch_shapes=[pltpu.VMEM((3, blk, NA_WIN_ROWS * GRID_W), F32)],
        compiler_params=pltpu.CompilerParams(
            dimension_semantics=("parallel", "arbitrary"),
            vmem_limit_bytes=VMEM_LIMIT),
        name="natten",
    )(u, knt, u, u, qg.reshape(1, NA_HD), tz)


def _merge_kernel(of_ref, ob_ref, hg_ref, bin_ref, ma0_ref, ma1_ref, mb0_ref, mb1_ref,
                  x_ref, p_ref, on_ref, pn_ref, gn_ref, wa_ref, wb_ref, wo_ref, wg_ref, wp_ref,
                  o_ref, *maybe_hn_ref):
    oa = of_ref[...].astype(F32) + ob_ref[...].astype(F32)
    gate_a = _silu(hg_ref[...].astype(F32))
    parts = []
    for h in range(HG_HEADS):
        hs = slice(h * HG_DK, (h + 1) * HG_DK)
        oh = oa[:, hs]
        ms = jnp.mean(oh * oh, axis=-1, keepdims=True)
        parts.append((oh * lax.rsqrt(ms + EPS) * on_ref[...] * gate_a[:, hs]).astype(BF16))
    ya = _dot(jnp.concatenate(parts, axis=1), wa_ref[...])
    yb = _dot(bin_ref[...], wb_ref[...])
    m_a = jnp.concatenate([ma0_ref[...], ma1_ref[...]], axis=1).astype(F32)
    m_b = jnp.concatenate([mb0_ref[...], mb1_ref[...]], axis=1).astype(F32)
    y = _sigmoid(m_a) * ya + _sigmoid(m_b) * yb
    x1 = x_ref[...] + _dot(y.astype(BF16), wo_ref[...])
    inv = lax.rsqrt(jnp.mean(x1 * x1, axis=-1, keepdims=True) + EPS)
    g = _sigmoid(_dot((x1 * pn_ref[...]).astype(BF16), wg_ref[...]) * inv)
    x2 = x1 + _dot(p_ref[...].astype(BF16), wp_ref[...]) * g
    o_ref[...] = x2
    for hn_ref in maybe_hn_ref:
        hn_ref[...] = _rms_bf16(x2, gn_ref[...])


def _merge(o_f, o_b, u, b_in, x, p, p_index, onorm, pnorm, next_norm, weights, cb_hgate, cb_ma, tm=MERGE_TM):
    s, d = x.shape
    tm = min(tm, s)
    pd = p.shape[-1]
    act = lambda wdt: pl.BlockSpec((tm, wdt), lambda i: (i, 0))
    ucol = lambda cb: pl.BlockSpec((tm, HG_W), lambda i, cb=cb: (i, cb))
    const = lambda shp: pl.BlockSpec(shp, lambda i: (0, 0), pipeline_mode=pl.Buffered(1))
    emit_next = next_norm is not None
    gn = next_norm if emit_next else pnorm
    out_specs = [act(d)] + ([act(d)] if emit_next else [])
    out_shape = [jax.ShapeDtypeStruct((s, d), F32)] + (
        [jax.ShapeDtypeStruct((s, d), BF16)] if emit_next else [])
    res = pl.pallas_call(
        _merge_kernel,
        grid=(s // tm,),
        in_specs=[act(HG_W), act(HG_W), ucol(cb_hgate), act(NA_W),
                  ucol(cb_ma), ucol(cb_ma + 1), ucol(cb_ma + 2), ucol(cb_ma + 3),
                  act(d), pl.BlockSpec((None, tm, pd), lambda i: (p_index, i, 0)),
                  const((1, HG_DK)), const((1, d)), const((1, d))]
                 + [const(w.shape) for w in weights],
        out_specs=out_specs,
        out_shape=out_shape,
        compiler_params=pltpu.CompilerParams(
            dimension_semantics=("parallel",),
            vmem_limit_bytes=VMEM_LIMIT),
        name="merge",
    )(o_f, o_b, u, b_in, u, u, u, u, x, p, onorm.reshape(1, HG_DK), pnorm.reshape(1, d),
      gn.reshape(1, d), *weights)
    return (res[0], res[1]) if emit_next else (res[0], None)


def kernel(x, p, norm_g, w_in, hgrn_lb, hgrn_onorm, na_qnorm, na_knorm, na_rpb,
           w_branch_a, w_branch_b, w_out, ple_norm, w_ple_gate, w_ple):
    bsz, s, d = x.shape
    depth = w_in.shape[0]
    lbp = jax.nn.softmax(hgrn_lb.astype(F32), axis=0)
    lower = jnp.cumsum(lbp, axis=0) - lbp[0:1]
    p_flat = p.reshape(depth * bsz, s, p.shape[-1])
    cb_hgate = 4 * HG_W // HG_W
    cb_na = 5 * HG_W // NA_HD
    cb_ma = (5 * HG_W + 4 * NA_W) // HG_W

    outs = []
    for bi in range(bsz):
        xb = x[bi]
        h = _rmsnorm(xb, norm_g[0])
        for i in range(depth):
            u = _in_proj(h, w_in, i)
            o_f, o_b, knt, weights = _hgrn(u, lower[i], na_knorm[i], (5 * HG_W + NA_W) // NA_W,
                                           (w_branch_a, w_branch_b, w_out, w_ple_gate), i)
            b_in = _natten(u, knt, na_qnorm[i], _natten_toeplitz(na_rpb[i]), col0=cb_na)
            xb, h = _merge(o_f, o_b, u, b_in, xb, p_flat, i * bsz + bi, hgrn_onorm[i], ple_norm[i],
                           norm_g[i + 1] if i + 1 < depth else None,
                           weights + [w_ple[i].astype(BF16)], cb_hgate=cb_hgate, cb_ma=cb_ma)
        outs.append(xb)
    return jnp.stack(outs, axis=0)
```

```python
import functools

import jax
import jax.numpy as jnp
import numpy as np
from jax import lax
from jax.experimental import pallas as pl
from jax.experimental.pallas import tpu as pltpu

F32 = jnp.float32
BF16 = jnp.bfloat16

GRID_W = 64
HG_HEADS = 8
HG_DK = 128
HG_W = HG_HEADS * HG_DK
NA_HEADS = 8
NA_HD = 128
NA_W = NA_HEADS * NA_HD
NA_KH = 8
NA_KW = 16
EPS = 1e-6
NEG = -1e30
F_MIN = 1e-6

CHUNK = 64
HG_CHUNKS_PER_STEP = 4
SUB = 8
NSUB = CHUNK // SUB
MID = SUB // 2

NA_ROWS_PER_STEP = 4
NA_WIN_ROWS = 12
NA_CHAINS = 16
NA_KT_TOKENS = 256
LOG2E = 1.4426950408889634

VMEM_LIMIT = 56 * 1024 * 1024
BF16_ROWS = 16

RMSNORM_TM = 1024
IN_PROJ_TM = 2048
IN_PROJ_TN = 1024
MERGE_TM = 256


def _dot(a, b):
    return jnp.dot(a, b, preferred_element_type=F32)


def _dot_nt(a, b):
    return lax.dot_general(a, b, (((1,), (1,)), ((), ())), preferred_element_type=F32)


def _sigmoid(x):
    return 1.0 / (1.0 + jnp.exp(-x))


def _silu(x):
    return x * _sigmoid(x)


def _layer_row(stack, layer):
    n = stack.shape[-1]
    return pl.BlockSpec((None, 1, n), lambda *_: (layer, 0, 0)), stack.reshape(stack.shape[0], 1, n)


def _rms_bf16(x, g):
    ms = jnp.mean(x * x, axis=-1, keepdims=True)
    return (x * lax.rsqrt(ms + EPS) * g).astype(BF16)


def _rmsnorm_kernel(x_ref, g_ref, o_ref):
    o_ref[...] = _rms_bf16(x_ref[...], g_ref[...])


def _rmsnorm(x, gains, layer, tm=RMSNORM_TM):
    s, d = x.shape
    tm = min(tm, s)
    g_spec, g = _layer_row(gains, layer)
    return pl.pallas_call(
        _rmsnorm_kernel,
        grid=(s // tm,),
        in_specs=[pl.BlockSpec((tm, d), lambda i: (i, 0)), g_spec],
        out_specs=pl.BlockSpec((tm, d), lambda i: (i, 0)),
        out_shape=jax.ShapeDtypeStruct((s, d), BF16),
        compiler_params=pltpu.CompilerParams(
            dimension_semantics=("parallel",),
            vmem_limit_bytes=VMEM_LIMIT),
        name="rmsnorm",
    )(x, g)


def _inproj_kernel(h_ref, w_ref, o_ref, wb_ref):
    @pl.when(pl.program_id(1) == 0)
    def _():
        wb_ref[...] = w_ref[...].astype(BF16)

    o_ref[...] = _dot(h_ref[...], wb_ref[...]).astype(o_ref.dtype)


def _in_proj(h, w, layer, tm=IN_PROJ_TM, tn=IN_PROJ_TN):
    s, d = h.shape
    n = w.shape[2]
    tm = min(tm, s)
    return pl.pallas_call(
        _inproj_kernel,
        grid=(n // tn, s // tm),
        in_specs=[
            pl.BlockSpec((tm, d), lambda j, i: (i, 0)),
            pl.BlockSpec((None, d, tn), lambda j, i: (layer, 0, j)),
        ],
        out_specs=pl.BlockSpec((tm, tn), lambda j, i: (i, j)),
        out_shape=jax.ShapeDtypeStruct((s, n), BF16),
        scratch_shapes=[pltpu.VMEM((d, tn), BF16)],
        compiler_params=pltpu.CompilerParams(
            dimension_semantics=("parallel", "arbitrary"),
            vmem_limit_bytes=VMEM_LIMIT),
        name="in_proj",
    )(h, w)


def _hgrn_selectors():
    c = CHUNK
    t = np.arange(c)[:, None]
    r = np.arange(c)[None, :]
    m = (t // SUB) * SUB + MID
    out = []
    for backward in (False, True):
        if backward:
            cum = lambda p: (r >= p).astype(np.float32)
            prev_m = m - SUB
        else:
            cum = lambda p: (r <= p).astype(np.float32)
            prev_m = m + SUB
        has_prev = (prev_m >= 0) & (prev_m < c)
        step = np.where(has_prev, cum(np.clip(prev_m, 0, c - 1)) - cum(m), 0.0)
        total = np.ones((BF16_ROWS, c), np.float32)
        sel = np.concatenate([cum(t) - cum(m), cum(m), 1.0 - cum(m), step, total], axis=0)
        out.append(np.concatenate([sel, sel], axis=1))
    return np.stack(out)


HG_SEL_ROWS = 4 * CHUNK + BF16_ROWS


def _hgrn_prologue(q_ref, v_ref, z_ref, rows, lb, sel, backward):
    c = CHUNK
    w = q_ref.shape[1]
    q = q_ref[rows, :].astype(F32)
    v = v_ref[rows, :]
    z = z_ref[rows, :].astype(F32)

    f = lb + (1.0 - lb) * _sigmoid(z)
    g = jnp.log2(jnp.maximum(f, F_MIN))
    k = 1.0 - f
    qs = q * _sigmoid(q)

    hi = g.astype(BF16)
    lo = (g - hi.astype(F32)).astype(BF16)
    cums = _dot(sel, jnp.concatenate([hi, lo], axis=0))
    e = jnp.exp2(cums[0:c])
    to_ref = jnp.exp2(cums[c:2 * c])
    from_ref = jnp.exp2(cums[2 * c:3 * c])
    step = jnp.exp2(cums[3 * c:4 * c])
    dec = jnp.exp2(cums[4 * c:4 * c + SUB])

    q0 = qs * e
    k0 = k * (1.0 / e)
    q_hat = (q0 * to_ref).astype(BF16)
    k_hat = (k0 * from_ref).astype(BF16)

    blocks = [slice(j * SUB, (j + 1) * SUB) for j in range(NSUB)]
    zero_blk = jnp.zeros((SUB, w), F32)
    order = range(NSUB) if backward else range(NSUB - 1, -1, -1)
    cur = [None] * NSUB
    q_tilde = [None] * NSUB
    k_tilde = [None] * NSUB
    for j in order:
        for jj in range(NSUB):
            if cur[jj] is not None:
                cur[jj] = cur[jj] * step[blocks[j]]
        cur[j] = q0[blocks[j]]
        q_tilde[j] = jnp.concatenate(
            [zero_blk if cur[jj] is None else cur[jj] for jj in range(NSUB)], axis=0).astype(BF16)
        k_tilde[j] = jnp.concatenate(
            [k0[blocks[jj]] if jj == j else zero_blk for jj in range(NSUB)], axis=0).astype(BF16)

    row = lax.broadcasted_iota(jnp.int32, (c, c), 0)
    col = lax.broadcasted_iota(jnp.int32, (c, c), 1)
    causal = (row >= col) if backward else (row <= col)
    heads = []
    for h in range(w // HG_DK):
        hs = slice(h * HG_DK, (h + 1) * HG_DK)
        heads.append(dict(
            q_hat=q_hat[:, hs], k_hat=k_hat[:, hs], v=v[:, hs], dec=dec[:, hs],
            q_cat=jnp.concatenate([q_tilde[j][:, hs] for j in range(NSUB)], axis=1),
            k_cat=jnp.concatenate([k_tilde[j][:, hs] for j in range(NSUB)], axis=1)))
    return heads, causal


def _hgrn_scores_t(q_cat, k_cat):
    rt = 2 * SUB
    kw = 2 * HG_DK
    return jnp.concatenate(
        [_dot_nt(k_cat[p * rt:(p + 1) * rt, p * kw:(p + 1) * kw], q_cat[:, p * kw:(p + 1) * kw])
         for p in range(CHUNK // rt)], axis=0)


def _hgrn_kernel(*refs, n_cast):
    (qf_ref, vf_ref, zf_ref, qb_ref, vb_ref, zb_ref, lb_ref, sel_ref, nk_ref, kg_ref) = refs[:10]
    refs = refs[10:]
    w32_refs, refs = refs[:n_cast], refs[n_cast:]
    (of_ref, ob_ref, knt_ref), refs = refs[:3], refs[3:]
    w16_refs, (stf_ref, stb_ref) = refs[:n_cast], refs[n_cast:]

    @pl.when(pl.program_id(0) == 0)
    def _():
        stf_ref[...] = jnp.zeros_like(stf_ref)
        stb_ref[...] = jnp.zeros_like(stb_ref)

    for w32_ref, w16_ref in zip(w32_refs, w16_refs):
        w16_ref[...] = w32_ref[...].astype(w16_ref.dtype)
    for h in range(knt_ref.shape[0]):
        kk = nk_ref[:, h * NA_HD:(h + 1) * NA_HD].astype(F32)
        knt_ref[h, 0] = _rms_bf16(kk, kg_ref[...]).T

    n_heads = stf_ref.shape[0]
    n_chunks = qf_ref.shape[0] // CHUNK
    rows_f = [slice(c * CHUNK, (c + 1) * CHUNK) for c in range(n_chunks)]
    rows_b = rows_f[::-1]
    pro = [(_hgrn_prologue(qf_ref, vf_ref, zf_ref, rf, lb_ref[0:1, :], sel_ref[0], False),
            _hgrn_prologue(qb_ref, vb_ref, zb_ref, rb, lb_ref[1:2, :], sel_ref[1], True))
           for rf, rb in zip(rows_f, rows_b)]

    states = [stf_ref[h] for h in range(n_heads)] + [stb_ref[h] for h in range(n_heads)]
    for (rf, rb), ((heads_f, causal_f), (heads_b, causal_b)) in zip(zip(rows_f, rows_b), pro):
        work = ([(heads_f[h], causal_f, of_ref, rf, h) for h in range(n_heads)]
                + [(heads_b[h], causal_b, ob_ref, rb, h) for h in range(n_heads)])
        atts = [_hgrn_scores_t(a["q_cat"], a["k_cat"]) for (a, _, _, _, _) in work]
        updates = [_dot(a["v"].T, a["k_hat"]) for (a, _, _, _, _) in work]
        atts = [jnp.where(causal, att, 0.0).astype(BF16).T
                for att, (_, causal, _, _, _) in zip(atts, work)]
        outs = [_dot(jnp.concatenate([a["q_hat"], att], axis=1),
                     jnp.concatenate([st.astype(BF16).T, a["v"]], axis=0))
                for att, st, (a, _, _, _, _) in zip(atts, states, work)]
        new_states = []
        for (a, _, o_ref, rows, h), st, out, upd in zip(work, states, outs, updates):
            o_ref[rows, h * HG_DK:(h + 1) * HG_DK] = out.astype(o_ref.dtype)
            decayed = st.reshape(HG_DK // SUB, SUB, HG_DK) * a["dec"][None]
            new_states.append(decayed.reshape(HG_DK, HG_DK) + upd)
        states = new_states
    for h in range(n_heads):
        stf_ref[h] = states[h]
        stb_ref[h] = states[n_heads + h]


def _hgrn(u, lb, kg, cb_nk, cast_weights=(), layer=0):
    s = u.shape[0]
    t = HG_CHUNKS_PER_STEP * CHUNK
    n = s // t
    assert s % t == 0 and NA_KT_TOKENS % t == 0 and t % 128 == 0
    per_tile = NA_KT_TOKENS // t
    fwd = lambda cb: pl.BlockSpec((t, HG_W), lambda i, cb=cb: (i, cb))
    bwd = lambda cb: pl.BlockSpec((t, HG_W), lambda i, cb=cb: (n - 1 - i, cb))
    heads = HG_W // HG_DK
    kg_spec, kg = _layer_row(kg, layer)
    cast_rows = [w.shape[1] // n for w in cast_weights]
    assert all(w.shape[1] == r * n and r % BF16_ROWS == 0 for w, r in zip(cast_weights, cast_rows))
    res = pl.pallas_call(
        functools.partial(_hgrn_kernel, n_cast=len(cast_weights)),
        grid=(n,),
        in_specs=[fwd(0), fwd(1), fwd(2), bwd(0), bwd(1), bwd(3),
                  pl.BlockSpec((None, 2, HG_W), lambda i: (layer, 0, 0)),
                  pl.BlockSpec((2, HG_SEL_ROWS, 2 * CHUNK), lambda i: (0, 0, 0)),
                  pl.BlockSpec((t, NA_W), lambda i: (i, cb_nk)),
                  kg_spec]
                 + [pl.BlockSpec((None, r, w.shape[2]), lambda i: (layer, i, 0))
                    for w, r in zip(cast_weights, cast_rows)],
        out_specs=[pl.BlockSpec((t, HG_W), lambda i: (i, 0)),
                   pl.BlockSpec((t, HG_W), lambda i: (n - 1 - i, 0)),
                   pl.BlockSpec((NA_HEADS, 1, NA_HD, t), lambda i: (0, i // per_tile, 0, i % per_tile))]
                  + [pl.BlockSpec((r, w.shape[2]), lambda i: (i, 0))
                     for w, r in zip(cast_weights, cast_rows)],
        out_shape=[jax.ShapeDtypeStruct((s, HG_W), BF16)] * 2
                  + [jax.ShapeDtypeStruct((NA_HEADS, s // NA_KT_TOKENS, NA_HD, NA_KT_TOKENS), BF16)]
                  + [jax.ShapeDtypeStruct(w.shape[1:], BF16) for w in cast_weights],
        scratch_shapes=[pltpu.VMEM((heads, HG_DK, HG_DK), F32)] * 2,
        compiler_params=pltpu.CompilerParams(
            dimension_semantics=("arbitrary",),
            vmem_limit_bytes=VMEM_LIMIT),
        name="hgrn",
    )(u, u, u, u, u, u, lb, jnp.asarray(_hgrn_selectors(), BF16), u, kg, *cast_weights)
    return res[0], res[1], res[2], list(res[3:])


NA_NEG_BLOCK = 2 * NA_KH - 1


def _natten_fill_bias(tz_ref, bias_ref, w0, r0, rows):
    kh = min(NA_KH, rows)
    lane = lax.broadcasted_iota(jnp.int32, (GRID_W, 2 * GRID_W), 1)

    def block_index(qi, a):
        r = r0 + qi
        r_start = min(max(r - kh // 2, 0), rows - kh)
        ar = w0 + a
        if r_start <= ar < r_start + kh:
            return ar - r + (NA_KH - 1)
        return NA_NEG_BLOCK

    for qi in range(NA_ROWS_PER_STEP):
        for ap in range(NA_WIN_ROWS // 2):
            ie, io = block_index(qi, 2 * ap), block_index(qi, 2 * ap + 1)
            blk = tz_ref[0, ie]
            if io != ie:
                blk = jnp.where(lane < GRID_W, blk, tz_ref[0, io])
            bias_ref[qi * GRID_W:(qi + 1) * GRID_W, ap * 2 * GRID_W:(ap + 1) * 2 * GRID_W] = blk


def _natten_kernel(q_ref, knt_ref, v_ref, gate_ref, qg_ref, tz_ref, o_ref, bias_ref):
    step = pl.program_id(1)
    s = v_ref.shape[0]
    rows = s // GRID_W
    rq, rk = NA_ROWS_PER_STEP, NA_WIN_ROWS
    n_blocks = rows // rq
    tq, win = rq * GRID_W, rk * GRID_W
    win_tiles = win // NA_KT_TOKENS

    @pl.when(step == 0)
    def _():
        _natten_fill_bias(tz_ref, bias_ref.at[0], 0, 0, rows)
        _natten_fill_bias(tz_ref, bias_ref.at[1], max(rq - NA_KH // 2, 0), rq, rows)
        _natten_fill_bias(tz_ref, bias_ref.at[2], rows - rk, rows - rq, rows)

    q = q_ref[...].astype(F32)
    ms = jnp.mean(q * q, axis=-1, keepdims=True)
    qn = (q * lax.rsqrt(ms + EPS) * (qg_ref[...] * (NA_HD ** -0.5 * LOG2E))).astype(BF16)

    qs, kws, vws, variants = [], [], [], []
    for c in range(NA_CHAINS):
        rb = step * NA_CHAINS + c
        w0 = jnp.clip(rq * rb - NA_KH // 2, 0, rows - rk) * GRID_W
        w0 = pl.multiple_of(w0, GRID_W)
        qs.append(qn[c * tq:(c + 1) * tq])
        kt = knt_ref[pl.ds(w0 // NA_KT_TOKENS, win_tiles)]
        kws.append(jnp.concatenate([kt[i] for i in range(win_tiles)], axis=1))
        vws.append(v_ref[pl.ds(w0, win), :])
        variants.append(jnp.where(rb == 0, 0, jnp.where(rb == n_blocks - 1, 2, 1)))
    scs = [_dot(qc, kw) for qc, kw in zip(qs, kws)]
    scs = [sc + bias_ref[var] for sc, var in zip(scs, variants)]
    ms = [jnp.max(sc, axis=-1, keepdims=True) for sc in scs]
    ps = [jnp.exp2(sc - m) for sc, m in zip(scs, ms)]
    ls = [jnp.sum(p, axis=-1, keepdims=True) for p in ps]
    outs = [_dot(p.astype(BF16), vw) for p, vw in zip(ps, vws)]
    for c, (o, l) in enumerate(zip(outs, ls)):
        gate = _silu(gate_ref[c * tq:(c + 1) * tq, :].astype(F32))
        o_ref[c * tq:(c + 1) * tq, :] = (o / l * gate).astype(o_ref.dtype)


def _natten_toeplitz(rpb):
    c = np.arange(GRID_W)
    c_start = np.clip(c - NA_KW // 2, 0, GRID_W - NA_KW)
    col_in = (c[None, :] >= c_start[:, None]) & (c[None, :] < c_start[:, None] + NA_KW)
    col_off = np.clip(c[None, :] - c[:, None] + (NA_KW - 1), 0, 2 * NA_KW - 2)
    sel = ((col_off[..., None] == np.arange(2 * NA_KW - 1)) & col_in[..., None]).astype(np.float32)
    sel = np.concatenate([sel, sel], axis=1) * LOG2E
    valid = np.concatenate([col_in, col_in], axis=1)[None] & (np.arange(2 * NA_KH) < NA_NEG_BLOCK)[:, None, None]
    neg = np.where(valid, 0.0, NEG).astype(np.float32)
    rpb = jnp.pad(rpb.astype(F32), ((0, 0), (0, 0), (0, 1), (0, 0)))
    return jnp.einsum('lhij,ckj->lhick', rpb, sel, precision=lax.Precision.HIGHEST) + neg


def _natten(u, knt, qg, tz, layer, col0):
    s = u.shape[0]
    blk = NA_ROWS_PER_STEP * GRID_W
    tq = NA_CHAINS * blk
    n_steps = s // tq
    assert s % tq == 0 and s // blk >= 3 and s // GRID_W >= NA_WIN_ROWS
    qg_spec, qg = _layer_row(qg, layer)
    assert blk % NA_KT_TOKENS == 0 and (NA_KH // 2 * GRID_W) % NA_KT_TOKENS == 0
    assert (NA_WIN_ROWS * GRID_W) % NA_KT_TOKENS == 0
    return pl.pallas_call(
        _natten_kernel,
        grid=(NA_HEADS, n_steps),
        in_specs=[
            pl.BlockSpec((tq, NA_HD), lambda h, rb: (rb, col0 + h)),
            pl.BlockSpec((None,) + knt.shape[1:], lambda h, rb: (h, 0, 0, 0)),
            pl.BlockSpec((s, NA_HD), lambda h, rb: (0, col0 + 2 * NA_HEADS + h)),
            pl.BlockSpec((tq, NA_HD), lambda h, rb: (rb, col0 + 3 * NA_HEADS + h)),
            qg_spec,
            pl.BlockSpec((None, 1) + tz.shape[2:], lambda h, rb: (layer, h, 0, 0, 0)),
        ],
        out_specs=pl.BlockSpec((tq, NA_HD), lambda h, rb: (rb, h)),
        out_shape=jax.ShapeDtypeStruct((s, NA_W), BF16),
        scratch_shapes=[pltpu.VMEM((3, blk, NA_WIN_ROWS * GRID_W), F32)],
        compiler_params=pltpu.CompilerParams(
            dimension_semantics=("parallel", "arbitrary"),
            vmem_limit_bytes=VMEM_LIMIT),
        name="natten",
    )(u, knt, u, u, qg, tz)


def _merge_kernel(of_ref, ob_ref, hg_ref, bin_ref, ma0_ref, ma1_ref, mb0_ref, mb1_ref,
                  x_ref, p_ref, on_ref, pn_ref, gn_ref, wa_ref, wb_ref, wo_ref, wg_ref, wp_ref,
                  o_ref, *maybe_hn_ref):
    oa = of_ref[...].astype(F32) + ob_ref[...].astype(F32)
    gate_a = _silu(hg_ref[...].astype(F32))
    parts = []
    for h in range(HG_HEADS):
        hs = slice(h * HG_DK, (h + 1) * HG_DK)
        oh = oa[:, hs]
        ms = jnp.mean(oh * oh, axis=-1, keepdims=True)
        parts.append((oh * lax.rsqrt(ms + EPS) * on_ref[...] * gate_a[:, hs]).astype(BF16))
    ya = _dot(jnp.concatenate(parts, axis=1), wa_ref[...])
    yb = _dot(bin_ref[...], wb_ref[...])
    m_a = jnp.concatenate([ma0_ref[...], ma1_ref[...]], axis=1).astype(F32)
    m_b = jnp.concatenate([mb0_ref[...], mb1_ref[...]], axis=1).astype(F32)
    y = _sigmoid(m_a) * ya + _sigmoid(m_b) * yb
    x1 = x_ref[...] + _dot(y.astype(BF16), wo_ref[...])
    inv = lax.rsqrt(jnp.mean(x1 * x1, axis=-1, keepdims=True) + EPS)
    g = _sigmoid(_dot((x1 * pn_ref[...]).astype(BF16), wg_ref[...]) * inv)
    x2 = x1 + _dot(p_ref[...].astype(BF16), wp_ref[...]) * g
    o_ref[...] = x2
    for hn_ref in maybe_hn_ref:
        hn_ref[...] = _rms_bf16(x2, gn_ref[...])


def _merge(o_f, o_b, u, b_in, x, p, p_index, onorm, pnorm, in_norm, layer, weights, cb_hgate, cb_ma,
           tm=MERGE_TM):
    s, d = x.shape
    tm = min(tm, s)
    pd = p.shape[-1]
    act = lambda wdt: pl.BlockSpec((tm, wdt), lambda i: (i, 0))
    ucol = lambda cb: pl.BlockSpec((tm, HG_W), lambda i, cb=cb: (i, cb))
    const = lambda shp: pl.BlockSpec(shp, lambda i: (0, 0), pipeline_mode=pl.Buffered(1))
    emit_next = layer + 1 < in_norm.shape[0]
    on_spec, onorm = _layer_row(onorm, layer)
    pn_spec, pnorm = _layer_row(pnorm, layer)
    gn_spec, in_norm = _layer_row(in_norm, layer + 1 if emit_next else layer)
    out_specs = [act(d)] + ([act(d)] if emit_next else [])
    out_shape = [jax.ShapeDtypeStruct((s, d), F32)] + (
        [jax.ShapeDtypeStruct((s, d), BF16)] if emit_next else [])
    res = pl.pallas_call(
        _merge_kernel,
        grid=(s // tm,),
        in_specs=[act(HG_W), act(HG_W), ucol(cb_hgate), act(NA_W),
                  ucol(cb_ma), ucol(cb_ma + 1), ucol(cb_ma + 2), ucol(cb_ma + 3),
                  act(d), pl.BlockSpec((None, tm, pd), lambda i: (p_index, i, 0)),
                  on_spec, pn_spec, gn_spec]
                 + [const(w.shape) for w in weights],
        out_specs=out_specs,
        out_shape=out_shape,
        compiler_params=pltpu.CompilerParams(
            dimension_semantics=("parallel",),
            vmem_limit_bytes=VMEM_LIMIT),
        name="merge",
    )(o_f, o_b, u, b_in, u, u, u, u, x, p, onorm, pnorm, in_norm, *weights)
    return (res[0], res[1]) if emit_next else (res[0], None)


def kernel(x, p, norm_g, w_in, hgrn_lb, hgrn_onorm, na_qnorm, na_knorm, na_rpb,
           w_branch_a, w_branch_b, w_out, ple_norm, w_ple_gate, w_ple):
    bsz, s, d = x.shape
    depth = w_in.shape[0]
    lbp = jax.nn.softmax(hgrn_lb.astype(F32), axis=0)
    lower = jnp.cumsum(lbp, axis=0) - lbp[0:1]
    p_flat = p.reshape(depth * bsz, s, p.shape[-1])
    tz = _natten_toeplitz(na_rpb)
    w_ple_bf16 = w_ple.astype(BF16)
    cb_hgate = 4 * HG_W // HG_W
    cb_na = 5 * HG_W // NA_HD
    cb_ma = (5 * HG_W + 4 * NA_W) // HG_W

    outs = []
    for bi in range(bsz):
        xb = x[bi]
        h = _rmsnorm(xb, norm_g, 0)
        for i in range(depth):
            u = _in_proj(h, w_in, i)
            o_f, o_b, knt, weights = _hgrn(u, lower, na_knorm, (5 * HG_W + NA_W) // NA_W,
                                           (w_branch_a, w_branch_b, w_out, w_ple_gate), i)
            b_in = _natten(u, knt, na_qnorm, tz, i, col0=cb_na)
            xb, h = _merge(o_f, o_b, u, b_in, xb, p_flat, i * bsz + bi, hgrn_onorm, ple_norm, norm_g, i,
                           weights + [w_ple_bf16[i]], cb_hgate=cb_hgate, cb_ma=cb_ma)
        outs.append(xb)
    return jnp.stack(outs, axis=0)
```

```python
import functools

import jax
import jax.numpy as jnp
import numpy as np
from jax import lax
from jax.experimental import pallas as pl
from jax.experimental.pallas import tpu as pltpu

F32 = jnp.float32
BF16 = jnp.bfloat16

GRID_W = 64
HG_HEADS = 8
HG_DK = 128
HG_W = HG_HEADS * HG_DK
NA_HEADS = 8
NA_HD = 128
NA_W = NA_HEADS * NA_HD
NA_KH = 8
NA_KW = 16
EPS = 1e-6
NEG = -1e30
F_MIN = 1e-6

CHUNK = 64
HG_CHUNKS_PER_STEP = 4
SUB = 8
NSUB = CHUNK // SUB
MID = SUB // 2

NA_ROWS_PER_STEP = 4
NA_WIN_ROWS = 12
NA_CHAINS = 16
NA_KT_TOKENS = 256
LOG2E = 1.4426950408889634

VMEM_LIMIT = 56 * 1024 * 1024
BF16_ROWS = 16

RMSNORM_TM = 1024
IN_PROJ_TM = 2048
IN_PROJ_TN = 1024
MERGE_TM = 256
MERGE_TN = 512


def _dot(a, b):
    return jnp.dot(a, b, preferred_element_type=F32)


def _dot_nt(a, b):
    return lax.dot_general(a, b, (((1,), (1,)), ((), ())), preferred_element_type=F32)


def _sigmoid(x):
    return 1.0 / (1.0 + jnp.exp(-x))


def _silu(x):
    return x * _sigmoid(x)


def _layer_row(stack, layer):
    n = stack.shape[-1]
    return pl.BlockSpec((None, 1, n), lambda *_: (layer, 0, 0)), stack.reshape(stack.shape[0], 1, n)


def _rms_bf16(x, g):
    ms = jnp.mean(x * x, axis=-1, keepdims=True)
    return (x * lax.rsqrt(ms + EPS) * g).astype(BF16)


def _rmsnorm_kernel(x_ref, g_ref, o_ref):
    o_ref[...] = _rms_bf16(x_ref[...], g_ref[...])


def _rmsnorm(x, gains, layer, tm=RMSNORM_TM):
    s, d = x.shape
    tm = min(tm, s)
    g_spec, g = _layer_row(gains, layer)
    return pl.pallas_call(
        _rmsnorm_kernel,
        grid=(s // tm,),
        in_specs=[pl.BlockSpec((tm, d), lambda i: (i, 0)), g_spec],
        out_specs=pl.BlockSpec((tm, d), lambda i: (i, 0)),
        out_shape=jax.ShapeDtypeStruct((s, d), BF16),
        compiler_params=pltpu.CompilerParams(
            dimension_semantics=("parallel",),
            vmem_limit_bytes=VMEM_LIMIT),
        name="rmsnorm",
    )(x, g)


def _inproj_kernel(h_ref, w_ref, o_ref, wb_ref):
    @pl.when(pl.program_id(1) == 0)
    def _():
        wb_ref[...] = w_ref[...].astype(BF16)

    o_ref[...] = _dot(h_ref[...], wb_ref[...]).astype(o_ref.dtype)


def _in_proj(h, w, layer, tm=IN_PROJ_TM, tn=IN_PROJ_TN):
    s, d = h.shape
    n = w.shape[2]
    tm = min(tm, s)
    return pl.pallas_call(
        _inproj_kernel,
        grid=(n // tn, s // tm),
        in_specs=[
            pl.BlockSpec((tm, d), lambda j, i: (i, 0)),
            pl.BlockSpec((None, d, tn), lambda j, i: (layer, 0, j)),
        ],
        out_specs=pl.BlockSpec((tm, tn), lambda j, i: (i, j)),
        out_shape=jax.ShapeDtypeStruct((s, n), BF16),
        scratch_shapes=[pltpu.VMEM((d, tn), BF16)],
        compiler_params=pltpu.CompilerParams(
            dimension_semantics=("parallel", "arbitrary"),
            vmem_limit_bytes=VMEM_LIMIT),
        name="in_proj",
    )(h, w)


def _hgrn_selectors():
    c = CHUNK
    t = np.arange(c)[:, None]
    r = np.arange(c)[None, :]
    m = (t // SUB) * SUB + MID
    out = []
    for backward in (False, True):
        if backward:
            cum = lambda p: (r >= p).astype(np.float32)
            prev_m = m - SUB
        else:
            cum = lambda p: (r <= p).astype(np.float32)
            prev_m = m + SUB
        has_prev = (prev_m >= 0) & (prev_m < c)
        step = np.where(has_prev, cum(np.clip(prev_m, 0, c - 1)) - cum(m), 0.0)
        total = np.ones((BF16_ROWS, c), np.float32)
        sel = np.concatenate([cum(t) - cum(m), cum(m), 1.0 - cum(m), step, total], axis=0)
        out.append(np.concatenate([sel, sel], axis=1))
    return np.stack(out)


HG_SEL_ROWS = 4 * CHUNK + BF16_ROWS


def _hgrn_prologue(q_ref, v_ref, z_ref, rows, lb, sel, backward):
    c = CHUNK
    w = q_ref.shape[1]
    q = q_ref[rows, :].astype(F32)
    v = v_ref[rows, :]
    z = z_ref[rows, :].astype(F32)

    f = lb + (1.0 - lb) * _sigmoid(z)
    g = jnp.log2(jnp.maximum(f, F_MIN))
    k = 1.0 - f
    qs = q * _sigmoid(q)

    hi = g.astype(BF16)
    lo = (g - hi.astype(F32)).astype(BF16)
    cums = _dot(sel, jnp.concatenate([hi, lo], axis=0))
    e = jnp.exp2(cums[0:c])
    to_ref = jnp.exp2(cums[c:2 * c])
    from_ref = jnp.exp2(cums[2 * c:3 * c])
    step = jnp.exp2(cums[3 * c:4 * c])
    dec = jnp.exp2(cums[4 * c:4 * c + SUB])

    q0 = qs * e
    k0 = k * (1.0 / e)
    q_hat = (q0 * to_ref).astype(BF16)
    k_hat = (k0 * from_ref).astype(BF16)

    blocks = [slice(j * SUB, (j + 1) * SUB) for j in range(NSUB)]
    zero_blk = jnp.zeros((SUB, w), F32)
    order = range(NSUB) if backward else range(NSUB - 1, -1, -1)
    cur = [None] * NSUB
    q_tilde = [None] * NSUB
    k_tilde = [None] * NSUB
    for j in order:
        for jj in range(NSUB):
            if cur[jj] is not None:
                cur[jj] = cur[jj] * step[blocks[j]]
        cur[j] = q0[blocks[j]]
        q_tilde[j] = jnp.concatenate(
            [zero_blk if cur[jj] is None else cur[jj] for jj in range(NSUB)], axis=0).astype(BF16)
        k_tilde[j] = jnp.concatenate(
            [k0[blocks[jj]] if jj == j else zero_blk for jj in range(NSUB)], axis=0).astype(BF16)

    row = lax.broadcasted_iota(jnp.int32, (c, c), 0)
    col = lax.broadcasted_iota(jnp.int32, (c, c), 1)
    causal = (row >= col) if backward else (row <= col)
    heads = []
    for h in range(w // HG_DK):
        hs = slice(h * HG_DK, (h + 1) * HG_DK)
        heads.append(dict(
            q_hat=q_hat[:, hs], k_hat=k_hat[:, hs], v=v[:, hs], dec=dec[:, hs],
            q_cat=jnp.concatenate([q_tilde[j][:, hs] for j in range(NSUB)], axis=1),
            k_cat=jnp.concatenate([k_tilde[j][:, hs] for j in range(NSUB)], axis=1)))
    return heads, causal


def _hgrn_scores_t(q_cat, k_cat):
    rt = 2 * SUB
    kw = 2 * HG_DK
    return jnp.concatenate(
        [_dot_nt(k_cat[p * rt:(p + 1) * rt, p * kw:(p + 1) * kw], q_cat[:, p * kw:(p + 1) * kw])
         for p in range(CHUNK // rt)], axis=0)


def _hgrn_kernel(*refs, n_cast):
    (qf_ref, vf_ref, zf_ref, qb_ref, vb_ref, zb_ref, lb_ref, sel_ref, nk_ref, kg_ref) = refs[:10]
    refs = refs[10:]
    w32_refs, refs = refs[:n_cast], refs[n_cast:]
    (of_ref, ob_ref, knt_ref), refs = refs[:3], refs[3:]
    w16_refs, (stf_ref, stb_ref) = refs[:n_cast], refs[n_cast:]

    @pl.when(pl.program_id(0) == 0)
    def _():
        stf_ref[...] = jnp.zeros_like(stf_ref)
        stb_ref[...] = jnp.zeros_like(stb_ref)

    for w32_ref, w16_ref in zip(w32_refs, w16_refs):
        w16_ref[...] = w32_ref[...].astype(w16_ref.dtype)
    for h in range(knt_ref.shape[0]):
        kk = nk_ref[:, h * NA_HD:(h + 1) * NA_HD].astype(F32)
        knt_ref[h, 0] = _rms_bf16(kk, kg_ref[...]).T

    n_heads = stf_ref.shape[0]
    n_chunks = qf_ref.shape[0] // CHUNK
    rows_f = [slice(c * CHUNK, (c + 1) * CHUNK) for c in range(n_chunks)]
    rows_b = rows_f[::-1]
    pro = [(_hgrn_prologue(qf_ref, vf_ref, zf_ref, rf, lb_ref[0:1, :], sel_ref[0], False),
            _hgrn_prologue(qb_ref, vb_ref, zb_ref, rb, lb_ref[1:2, :], sel_ref[1], True))
           for rf, rb in zip(rows_f, rows_b)]

    states = [stf_ref[h] for h in range(n_heads)] + [stb_ref[h] for h in range(n_heads)]
    for (rf, rb), ((heads_f, causal_f), (heads_b, causal_b)) in zip(zip(rows_f, rows_b), pro):
        work = ([(heads_f[h], causal_f, of_ref, rf, h) for h in range(n_heads)]
                + [(heads_b[h], causal_b, ob_ref, rb, h) for h in range(n_heads)])
        atts = [_hgrn_scores_t(a["q_cat"], a["k_cat"]) for (a, _, _, _, _) in work]
        updates = [_dot(a["v"].T, a["k_hat"]) for (a, _, _, _, _) in work]
        atts = [jnp.where(causal, att, 0.0).astype(BF16).T
                for att, (_, causal, _, _, _) in zip(atts, work)]
        outs = [_dot(jnp.concatenate([a["q_hat"], att], axis=1),
                     jnp.concatenate([st.astype(BF16).T, a["v"]], axis=0))
                for att, st, (a, _, _, _, _) in zip(atts, states, work)]
        new_states = []
        for (a, _, o_ref, rows, h), st, out, upd in zip(work, states, outs, updates):
            o_ref[rows, h * HG_DK:(h + 1) * HG_DK] = out.astype(o_ref.dtype)
            decayed = st.reshape(HG_DK // SUB, SUB, HG_DK) * a["dec"][None]
            new_states.append(decayed.reshape(HG_DK, HG_DK) + upd)
        states = new_states
    for h in range(n_heads):
        stf_ref[h] = states[h]
        stb_ref[h] = states[n_heads + h]


def _hgrn(u, lb, kg, cb_nk, cast_weights=(), layer=0):
    s = u.shape[0]
    t = HG_CHUNKS_PER_STEP * CHUNK
    n = s // t
    assert s % t == 0 and NA_KT_TOKENS % t == 0 and t % 128 == 0
    per_tile = NA_KT_TOKENS // t
    fwd = lambda cb: pl.BlockSpec((t, HG_W), lambda i, cb=cb: (i, cb))
    bwd = lambda cb: pl.BlockSpec((t, HG_W), lambda i, cb=cb: (n - 1 - i, cb))
    heads = HG_W // HG_DK
    kg_spec, kg = _layer_row(kg, layer)
    cast_rows = [w.shape[1] // n for w in cast_weights]
    assert all(w.shape[1] == r * n and r % BF16_ROWS == 0 for w, r in zip(cast_weights, cast_rows))
    res = pl.pallas_call(
        functools.partial(_hgrn_kernel, n_cast=len(cast_weights)),
        grid=(n,),
        in_specs=[fwd(0), fwd(1), fwd(2), bwd(0), bwd(1), bwd(3),
                  pl.BlockSpec((None, 2, HG_W), lambda i: (layer, 0, 0)),
                  pl.BlockSpec((2, HG_SEL_ROWS, 2 * CHUNK), lambda i: (0, 0, 0)),
                  pl.BlockSpec((t, NA_W), lambda i: (i, cb_nk)),
                  kg_spec]
                 + [pl.BlockSpec((None, r, w.shape[2]), lambda i: (layer, i, 0))
                    for w, r in zip(cast_weights, cast_rows)],
        out_specs=[pl.BlockSpec((t, HG_W), lambda i: (i, 0)),
                   pl.BlockSpec((t, HG_W), lambda i: (n - 1 - i, 0)),
                   pl.BlockSpec((NA_HEADS, 1, NA_HD, t), lambda i: (0, i // per_tile, 0, i % per_tile))]
                  + [pl.BlockSpec((r, w.shape[2]), lambda i: (i, 0))
                     for w, r in zip(cast_weights, cast_rows)],
        out_shape=[jax.ShapeDtypeStruct((s, HG_W), BF16)] * 2
                  + [jax.ShapeDtypeStruct((NA_HEADS, s // NA_KT_TOKENS, NA_HD, NA_KT_TOKENS), BF16)]
                  + [jax.ShapeDtypeStruct(w.shape[1:], BF16) for w in cast_weights],
        scratch_shapes=[pltpu.VMEM((heads, HG_DK, HG_DK), F32)] * 2,
        compiler_params=pltpu.CompilerParams(
            dimension_semantics=("arbitrary",),
            vmem_limit_bytes=VMEM_LIMIT),
        name="hgrn",
    )(u, u, u, u, u, u, lb, jnp.asarray(_hgrn_selectors(), BF16), u, kg, *cast_weights)
    return res[0], res[1], res[2], list(res[3:])


NA_NEG_BLOCK = 2 * NA_KH - 1


def _natten_fill_bias(tz_ref, bias_ref, w0, r0, rows):
    kh = min(NA_KH, rows)
    lane = lax.broadcasted_iota(jnp.int32, (GRID_W, 2 * GRID_W), 1)

    def block_index(qi, a):
        r = r0 + qi
        r_start = min(max(r - kh // 2, 0), rows - kh)
        ar = w0 + a
        if r_start <= ar < r_start + kh:
            return ar - r + (NA_KH - 1)
        return NA_NEG_BLOCK

    for qi in range(NA_ROWS_PER_STEP):
        for ap in range(NA_WIN_ROWS // 2):
            ie, io = block_index(qi, 2 * ap), block_index(qi, 2 * ap + 1)
            blk = tz_ref[0, ie]
            if io != ie:
                blk = jnp.where(lane < GRID_W, blk, tz_ref[0, io])
            bias_ref[qi * GRID_W:(qi + 1) * GRID_W, ap * 2 * GRID_W:(ap + 1) * 2 * GRID_W] = blk


def _natten_kernel(q_ref, knt_ref, v_ref, gate_ref, qg_ref, tz_ref, o_ref, bias_ref):
    step = pl.program_id(1)
    s = v_ref.shape[0]
    rows = s // GRID_W
    rq, rk = NA_ROWS_PER_STEP, NA_WIN_ROWS
    n_blocks = rows // rq
    tq, win = rq * GRID_W, rk * GRID_W
    win_tiles = win // NA_KT_TOKENS

    @pl.when(step == 0)
    def _():
        _natten_fill_bias(tz_ref, bias_ref.at[0], 0, 0, rows)
        _natten_fill_bias(tz_ref, bias_ref.at[1], max(rq - NA_KH // 2, 0), rq, rows)
        _natten_fill_bias(tz_ref, bias_ref.at[2], rows - rk, rows - rq, rows)

    q = q_ref[...].astype(F32)
    ms = jnp.mean(q * q, axis=-1, keepdims=True)
    qn = (q * lax.rsqrt(ms + EPS) * (qg_ref[...] * (NA_HD ** -0.5 * LOG2E))).astype(BF16)

    qs, kws, vws, variants = [], [], [], []
    for c in range(NA_CHAINS):
        rb = step * NA_CHAINS + c
        w0 = jnp.clip(rq * rb - NA_KH // 2, 0, rows - rk) * GRID_W
        w0 = pl.multiple_of(w0, GRID_W)
        qs.append(qn[c * tq:(c + 1) * tq])
        kt = knt_ref[pl.ds(w0 // NA_KT_TOKENS, win_tiles)]
        kws.append(jnp.concatenate([kt[i] for i in range(win_tiles)], axis=1))
        vws.append(v_ref[pl.ds(w0, win), :])
        variants.append(jnp.where(rb == 0, 0, jnp.where(rb == n_blocks - 1, 2, 1)))
    scs = [_dot(qc, kw) for qc, kw in zip(qs, kws)]
    scs = [sc + bias_ref[var] for sc, var in zip(scs, variants)]
    ms = [jnp.max(sc, axis=-1, keepdims=True) for sc in scs]
    ps = [jnp.exp2(sc - m) for sc, m in zip(scs, ms)]
    ls = [jnp.sum(p, axis=-1, keepdims=True) for p in ps]
    outs = [_dot(p.astype(BF16), vw) for p, vw in zip(ps, vws)]
    for c, (o, l) in enumerate(zip(outs, ls)):
        gate = _silu(gate_ref[c * tq:(c + 1) * tq, :].astype(F32))
        o_ref[c * tq:(c + 1) * tq, :] = (o / l * gate).astype(o_ref.dtype)


def _natten_toeplitz(rpb):
    c = np.arange(GRID_W)
    c_start = np.clip(c - NA_KW // 2, 0, GRID_W - NA_KW)
    col_in = (c[None, :] >= c_start[:, None]) & (c[None, :] < c_start[:, None] + NA_KW)
    col_off = np.clip(c[None, :] - c[:, None] + (NA_KW - 1), 0, 2 * NA_KW - 2)
    sel = ((col_off[..., None] == np.arange(2 * NA_KW - 1)) & col_in[..., None]).astype(np.float32)
    sel = np.concatenate([sel, sel], axis=1) * LOG2E
    valid = np.concatenate([col_in, col_in], axis=1)[None] & (np.arange(2 * NA_KH) < NA_NEG_BLOCK)[:, None, None]
    neg = np.where(valid, 0.0, NEG).astype(np.float32)
    rpb = jnp.pad(rpb.astype(F32), ((0, 0), (0, 0), (0, 1), (0, 0)))
    return jnp.einsum('lhij,ckj->lhick', rpb, sel, precision=lax.Precision.HIGHEST) + neg


def _natten(u, knt, qg, tz, layer, col0):
    s = u.shape[0]
    blk = NA_ROWS_PER_STEP * GRID_W
    tq = NA_CHAINS * blk
    n_steps = s // tq
    assert s % tq == 0 and s // blk >= 3 and s // GRID_W >= NA_WIN_ROWS
    qg_spec, qg = _layer_row(qg, layer)
    assert blk % NA_KT_TOKENS == 0 and (NA_KH // 2 * GRID_W) % NA_KT_TOKENS == 0
    assert (NA_WIN_ROWS * GRID_W) % NA_KT_TOKENS == 0
    return pl.pallas_call(
        _natten_kernel,
        grid=(NA_HEADS, n_steps),
        in_specs=[
            pl.BlockSpec((tq, NA_HD), lambda h, rb: (rb, col0 + h)),
            pl.BlockSpec((None,) + knt.shape[1:], lambda h, rb: (h, 0, 0, 0)),
            pl.BlockSpec((s, NA_HD), lambda h, rb: (0, col0 + 2 * NA_HEADS + h)),
            pl.BlockSpec((tq, NA_HD), lambda h, rb: (rb, col0 + 3 * NA_HEADS + h)),
            qg_spec,
            pl.BlockSpec((None, 1) + tz.shape[2:], lambda h, rb: (layer, h, 0, 0, 0)),
        ],
        out_specs=pl.BlockSpec((tq, NA_HD), lambda h, rb: (rb, h)),
        out_shape=jax.ShapeDtypeStruct((s, NA_W), BF16),
        scratch_shapes=[pltpu.VMEM((3, blk, NA_WIN_ROWS * GRID_W), F32)],
        compiler_params=pltpu.CompilerParams(
            dimension_semantics=("parallel", "arbitrary"),
            vmem_limit_bytes=VMEM_LIMIT),
        name="natten",
    )(u, knt, u, u, qg, tz)


def _merge_kernel(of_ref, ob_ref, hg_ref, bin_ref, ma0_ref, ma1_ref, mb0_ref, mb1_ref,
                  x_ref, p_ref, on_ref, pn_ref, gn_ref, wa_ref, wb_ref, wo_ref, wg_ref, wp_ref,
                  o_ref, *maybe_hn_ref):
    oa = of_ref[...].astype(F32) + ob_ref[...].astype(F32)
    gate_a = _silu(hg_ref[...].astype(F32))
    parts = []
    for h in range(HG_HEADS):
        hs = slice(h * HG_DK, (h + 1) * HG_DK)
        oh = oa[:, hs]
        ms = jnp.mean(oh * oh, axis=-1, keepdims=True)
        parts.append((oh * lax.rsqrt(ms + EPS) * on_ref[...] * gate_a[:, hs]).astype(BF16))
    a_in = jnp.concatenate(parts, axis=1)
    b_in = bin_ref[...]
    y_parts = []
    for t in range(0, wo_ref.shape[0], MERGE_TN):
        ya = _dot(a_in, wa_ref[:, t:t + MERGE_TN])
        yb = _dot(b_in, wb_ref[:, t:t + MERGE_TN])
        ma_ref = ma0_ref if t < HG_W else ma1_ref
        mb_ref = mb0_ref if t < HG_W else mb1_ref
        tt = t % HG_W
        m_a = ma_ref[:, tt:tt + MERGE_TN].astype(F32)
        m_b = mb_ref[:, tt:tt + MERGE_TN].astype(F32)
        y_parts.append((_sigmoid(m_a) * ya + _sigmoid(m_b) * yb).astype(BF16))
    x1 = x_ref[...] + _dot(jnp.concatenate(y_parts, axis=1), wo_ref[...])
    inv = lax.rsqrt(jnp.mean(x1 * x1, axis=-1, keepdims=True) + EPS)
    xg = (x1 * pn_ref[...]).astype(BF16)
    pb = p_ref[...].astype(BF16)
    x2_parts = []
    for t in range(0, wg_ref.shape[1], MERGE_TN):
        g = _sigmoid(_dot(xg, wg_ref[:, t:t + MERGE_TN]) * inv)
        x2_parts.append(x1[:, t:t + MERGE_TN] + _dot(pb, wp_ref[:, t:t + MERGE_TN]) * g)
    x2 = jnp.concatenate(x2_parts, axis=1)
    o_ref[...] = x2
    for hn_ref in maybe_hn_ref:
        hn_ref[...] = _rms_bf16(x2, gn_ref[...])


def _merge(o_f, o_b, u, b_in, x, p, p_index, onorm, pnorm, in_norm, layer, weights, cb_hgate, cb_ma,
           tm=MERGE_TM):
    s, d = x.shape
    tm = min(tm, s)
    pd = p.shape[-1]
    act = lambda wdt: pl.BlockSpec((tm, wdt), lambda i: (i, 0))
    ucol = lambda cb: pl.BlockSpec((tm, HG_W), lambda i, cb=cb: (i, cb))
    const = lambda shp: pl.BlockSpec(shp, lambda i: (0, 0), pipeline_mode=pl.Buffered(1))
    emit_next = layer + 1 < in_norm.shape[0]
    on_spec, onorm = _layer_row(onorm, layer)
    pn_spec, pnorm = _layer_row(pnorm, layer)
    gn_spec, in_norm = _layer_row(in_norm, layer + 1 if emit_next else layer)
    out_specs = [act(d)] + ([act(d)] if emit_next else [])
    out_shape = [jax.ShapeDtypeStruct((s, d), F32)] + (
        [jax.ShapeDtypeStruct((s, d), BF16)] if emit_next else [])
    res = pl.pallas_call(
        _merge_kernel,
        grid=(s // tm,),
        in_specs=[act(HG_W), act(HG_W), ucol(cb_hgate), act(NA_W),
                  ucol(cb_ma), ucol(cb_ma + 1), ucol(cb_ma + 2), ucol(cb_ma + 3),
                  act(d), pl.BlockSpec((None, tm, pd), lambda i: (p_index, i, 0)),
                  on_spec, pn_spec, gn_spec]
                 + [const(w.shape) for w in weights],
        out_specs=out_specs,
        out_shape=out_shape,
        compiler_params=pltpu.CompilerParams(
            dimension_semantics=("parallel",),
            vmem_limit_bytes=VMEM_LIMIT),
        name="merge",
    )(o_f, o_b, u, b_in, u, u, u, u, x, p, onorm, pnorm, in_norm, *weights)
    return (res[0], res[1]) if emit_next else (res[0], None)


def kernel(x, p, norm_g, w_in, hgrn_lb, hgrn_onorm, na_qnorm, na_knorm, na_rpb,
           w_branch_a, w_branch_b, w_out, ple_norm, w_ple_gate, w_ple):
    bsz, s, d = x.shape
    depth = w_in.shape[0]
    lbp = jax.nn.softmax(hgrn_lb.astype(F32), axis=0)
    lower = jnp.cumsum(lbp, axis=0) - lbp[0:1]
    p_flat = p.reshape(depth * bsz, s, p.shape[-1])
    tz = _natten_toeplitz(na_rpb)
    w_ple_bf16 = w_ple.astype(BF16)
    cb_hgate = 4 * HG_W // HG_W
    cb_na = 5 * HG_W // NA_HD
    cb_ma = (5 * HG_W + 4 * NA_W) // HG_W

    outs = []
    for bi in range(bsz):
        xb = x[bi]
        h = _rmsnorm(xb, norm_g, 0)
        for i in range(depth):
            u = _in_proj(h, w_in, i)
            o_f, o_b, knt, weights = _hgrn(u, lower, na_knorm, (5 * HG_W + NA_W) // NA_W,
                                           (w_branch_a, w_branch_b, w_out, w_ple_gate), i)
            b_in = _natten(u, knt, na_qnorm, tz, i, col0=cb_na)
            xb, h = _merge(o_f, o_b, u, b_in, xb, p_flat, i * bsz + bi, hgrn_onorm, ple_norm, norm_g, i,
                           weights + [w_ple_bf16[i]], cb_hgate=cb_hgate, cb_ma=cb_ma)
        outs.append(xb)
    return jnp.stack(outs, axis=0)
```

```python
import functools

import jax
import jax.numpy as jnp
import numpy as np
from jax import lax
from jax.experimental import pallas as pl
from jax.experimental.pallas import tpu as pltpu

F32 = jnp.float32
BF16 = jnp.bfloat16

GRID_W = 64
HG_HEADS = 8
HG_DK = 128
HG_W = HG_HEADS * HG_DK
NA_HEADS = 8
NA_HD = 128
NA_W = NA_HEADS * NA_HD
NA_KH = 8
NA_KW = 16
EPS = 1e-6
NEG = -1e30
F_MIN = 1e-6

CHUNK = 64
HG_CHUNKS_PER_STEP = 4
SUB = 8
NSUB = CHUNK // SUB
MID = SUB // 2

NA_ROWS_PER_STEP = 4
NA_WIN_ROWS = 12
NA_CHAINS = 16
NA_KT_TOKENS = 256
LOG2E = 1.4426950408889634

VMEM_LIMIT = 56 * 1024 * 1024
BF16_ROWS = 16

RMSNORM_TM = 1024
IN_PROJ_TM = 2048
IN_PROJ_TN = 1024
MERGE_TM = 256
MERGE_TN = 512


def _dot(a, b):
    return jnp.dot(a, b, preferred_element_type=F32)


def _dot_nt(a, b):
    return lax.dot_general(a, b, (((1,), (1,)), ((), ())), preferred_element_type=F32)


def _sigmoid(x):
    return 1.0 / (1.0 + jnp.exp(-x))


def _silu(x):
    return x * _sigmoid(x)


def _layer_row(stack, layer):
    n = stack.shape[-1]
    return pl.BlockSpec((None, 1, n), lambda *_: (layer, 0, 0)), stack.reshape(stack.shape[0], 1, n)


def _rms_bf16(x, g):
    ms = jnp.mean(x * x, axis=-1, keepdims=True)
    return (x * lax.rsqrt(ms + EPS) * g).astype(BF16)


def _rmsnorm_kernel(x_ref, g_ref, o_ref):
    o_ref[...] = _rms_bf16(x_ref[...], g_ref[...])


def _rmsnorm(x, gains, layer, tm=RMSNORM_TM):
    s, d = x.shape
    tm = min(tm, s)
    g_spec, g = _layer_row(gains, layer)
    return pl.pallas_call(
        _rmsnorm_kernel,
        grid=(s // tm,),
        in_specs=[pl.BlockSpec((tm, d), lambda i: (i, 0)), g_spec],
        out_specs=pl.BlockSpec((tm, d), lambda i: (i, 0)),
        out_shape=jax.ShapeDtypeStruct((s, d), BF16),
        compiler_params=pltpu.CompilerParams(
            dimension_semantics=("parallel",),
            vmem_limit_bytes=VMEM_LIMIT),
        name="rmsnorm",
    )(x, g)


def _inproj_kernel(h_ref, w_ref, o_ref, wb_ref):
    @pl.when(pl.program_id(1) == 0)
    def _():
        wb_ref[...] = w_ref[...].astype(BF16)

    o_ref[...] = _dot(h_ref[...], wb_ref[...]).astype(o_ref.dtype)


def _in_proj(h, w, layer, tm=IN_PROJ_TM, tn=IN_PROJ_TN):
    s, d = h.shape
    n = w.shape[2]
    tm = min(tm, s)
    return pl.pallas_call(
        _inproj_kernel,
        grid=(n // tn, s // tm),
        in_specs=[
            pl.BlockSpec((tm, d), lambda j, i: (i, 0)),
            pl.BlockSpec((None, d, tn), lambda j, i: (layer, 0, j)),
        ],
        out_specs=pl.BlockSpec((tm, tn), lambda j, i: (i, j)),
        out_shape=jax.ShapeDtypeStruct((s, n), BF16),
        scratch_shapes=[pltpu.VMEM((d, tn), BF16)],
        compiler_params=pltpu.CompilerParams(
            dimension_semantics=("parallel", "arbitrary"),
            vmem_limit_bytes=VMEM_LIMIT),
        name="in_proj",
    )(h, w)


def _hgrn_selectors():
    c = CHUNK
    t = np.arange(c)[:, None]
    r = np.arange(c)[None, :]
    m = (t // SUB) * SUB + MID
    out = []
    for backward in (False, True):
        if backward:
            cum = lambda p: (r >= p).astype(np.float32)
            prev_m = m - SUB
        else:
            cum = lambda p: (r <= p).astype(np.float32)
            prev_m = m + SUB
        has_prev = (prev_m >= 0) & (prev_m < c)
        step = np.where(has_prev, cum(np.clip(prev_m, 0, c - 1)) - cum(m), 0.0)
        total = np.ones((BF16_ROWS, c), np.float32)
        sel = np.concatenate([cum(t) - cum(m), cum(m), 1.0 - cum(m), step, total], axis=0)
        out.append(np.concatenate([sel, sel], axis=1))
    return np.stack(out)


HG_SEL_ROWS = 4 * CHUNK + BF16_ROWS


def _hgrn_prologue(q_ref, v_ref, z_ref, rows, lb, sel, backward):
    c = CHUNK
    w = q_ref.shape[1]
    q = q_ref[rows, :].astype(F32)
    v = v_ref[rows, :]
    z = z_ref[rows, :].astype(F32)

    f = lb + (1.0 - lb) * _sigmoid(z)
    g = jnp.log2(jnp.maximum(f, F_MIN))
    k = 1.0 - f
    qs = q * _sigmoid(q)

    hi = g.astype(BF16)
    lo = (g - hi.astype(F32)).astype(BF16)
    cums = _dot(sel, jnp.concatenate([hi, lo], axis=0))
    e = jnp.exp2(cums[0:c])
    to_ref = jnp.exp2(cums[c:2 * c])
    from_ref = jnp.exp2(cums[2 * c:3 * c])
    step = jnp.exp2(cums[3 * c:4 * c])
    dec = jnp.exp2(cums[4 * c:4 * c + SUB])

    q0 = qs * e
    k0 = k * (1.0 / e)
    q_hat = (q0 * to_ref).astype(BF16)
    k_hat = (k0 * from_ref).astype(BF16)

    blocks = [slice(j * SUB, (j + 1) * SUB) for j in range(NSUB)]
    zero_blk = jnp.zeros((SUB, w), F32)
    order = range(NSUB) if backward else range(NSUB - 1, -1, -1)
    cur = [None] * NSUB
    q_tilde = [None] * NSUB
    k_tilde = [None] * NSUB
    for j in order:
        for jj in range(NSUB):
            if cur[jj] is not None:
                cur[jj] = cur[jj] * step[blocks[j]]
        cur[j] = q0[blocks[j]]
        q_tilde[j] = jnp.concatenate(
            [zero_blk if cur[jj] is None else cur[jj] for jj in range(NSUB)], axis=0).astype(BF16)
        k_tilde[j] = jnp.concatenate(
            [k0[blocks[jj]] if jj == j else zero_blk for jj in range(NSUB)], axis=0).astype(BF16)

    row = lax.broadcasted_iota(jnp.int32, (c, c), 0)
    col = lax.broadcasted_iota(jnp.int32, (c, c), 1)
    causal = (row >= col) if backward else (row <= col)
    heads = []
    for h in range(w // HG_DK):
        hs = slice(h * HG_DK, (h + 1) * HG_DK)
        heads.append(dict(
            q_hat=q_hat[:, hs], k_hat=k_hat[:, hs], v=v[:, hs], dec=dec[:, hs], backward=backward,
            q_cat=jnp.concatenate([q_tilde[j][:, hs] for j in range(NSUB)], axis=1),
            k_cat=jnp.concatenate([k_tilde[j][:, hs] for j in range(NSUB)], axis=1)))
    return heads, causal


def _hgrn_scores_t(q_cat, k_cat, backward):
    rt = 2 * SUB
    kw = 2 * HG_DK
    rows = []
    for p in range(CHUNK // rt):
        t_rows = slice(0, (p + 1) * rt) if backward else slice(p * rt, CHUNK)
        part = _dot_nt(k_cat[p * rt:(p + 1) * rt, p * kw:(p + 1) * kw], q_cat[t_rows, p * kw:(p + 1) * kw])
        pad = CHUNK - part.shape[1]
        if pad:
            zero = jnp.zeros((rt, pad), F32)
            part = jnp.concatenate([part, zero] if backward else [zero, part], axis=1)
        rows.append(part)
    return jnp.concatenate(rows, axis=0)


def _hgrn_kernel(*refs, n_cast):
    (qf_ref, vf_ref, zf_ref, qb_ref, vb_ref, zb_ref, lb_ref, sel_ref, nk_ref, kg_ref) = refs[:10]
    refs = refs[10:]
    w32_refs, refs = refs[:n_cast], refs[n_cast:]
    (of_ref, ob_ref, knt_ref), refs = refs[:3], refs[3:]
    w16_refs, (stf_ref, stb_ref) = refs[:n_cast], refs[n_cast:]

    @pl.when(pl.program_id(0) == 0)
    def _():
        stf_ref[...] = jnp.zeros_like(stf_ref)
        stb_ref[...] = jnp.zeros_like(stb_ref)

    for w32_ref, w16_ref in zip(w32_refs, w16_refs):
        w16_ref[...] = w32_ref[...].astype(w16_ref.dtype)
    for h in range(knt_ref.shape[0]):
        kk = nk_ref[:, h * NA_HD:(h + 1) * NA_HD].astype(F32)
        knt_ref[h, 0] = _rms_bf16(kk, kg_ref[...]).T

    n_heads = stf_ref.shape[0]
    n_chunks = qf_ref.shape[0] // CHUNK
    rows_f = [slice(c * CHUNK, (c + 1) * CHUNK) for c in range(n_chunks)]
    rows_b = rows_f[::-1]
    pro = [(_hgrn_prologue(qf_ref, vf_ref, zf_ref, rf, lb_ref[0:1, :], sel_ref[0], False),
            _hgrn_prologue(qb_ref, vb_ref, zb_ref, rb, lb_ref[1:2, :], sel_ref[1], True))
           for rf, rb in zip(rows_f, rows_b)]

    states = [stf_ref[h] for h in range(n_heads)] + [stb_ref[h] for h in range(n_heads)]
    for (rf, rb), ((heads_f, causal_f), (heads_b, causal_b)) in zip(zip(rows_f, rows_b), pro):
        work = ([(heads_f[h], causal_f, of_ref, rf, h) for h in range(n_heads)]
                + [(heads_b[h], causal_b, ob_ref, rb, h) for h in range(n_heads)])
        atts = [_hgrn_scores_t(a["q_cat"], a["k_cat"], a["backward"]) for (a, _, _, _, _) in work]
        updates = [_dot(a["v"].T, a["k_hat"]) for (a, _, _, _, _) in work]
        atts = [jnp.where(causal, att, 0.0).astype(BF16).T
                for att, (_, causal, _, _, _) in zip(atts, work)]
        outs = [_dot(jnp.concatenate([a["q_hat"], att], axis=1),
                     jnp.concatenate([st.astype(BF16).T, a["v"]], axis=0))
                for att, st, (a, _, _, _, _) in zip(atts, states, work)]
        new_states = []
        for (a, _, o_ref, rows, h), st, out, upd in zip(work, states, outs, updates):
            o_ref[rows, h * HG_DK:(h + 1) * HG_DK] = out.astype(o_ref.dtype)
            decayed = st.reshape(HG_DK // SUB, SUB, HG_DK) * a["dec"][None]
            new_states.append(decayed.reshape(HG_DK, HG_DK) + upd)
        states = new_states
    for h in range(n_heads):
        stf_ref[h] = states[h]
        stb_ref[h] = states[n_heads + h]


def _hgrn(u, lb, kg, cb_nk, cast_weights=(), layer=0):
    s = u.shape[0]
    t = HG_CHUNKS_PER_STEP * CHUNK
    n = s // t
    assert s % t == 0 and NA_KT_TOKENS % t == 0 and t % 128 == 0
    per_tile = NA_KT_TOKENS // t
    fwd = lambda cb: pl.BlockSpec((t, HG_W), lambda i, cb=cb: (i, cb))
    bwd = lambda cb: pl.BlockSpec((t, HG_W), lambda i, cb=cb: (n - 1 - i, cb))
    heads = HG_W // HG_DK
    kg_spec, kg = _layer_row(kg, layer)
    cast_rows = [w.shape[1] // n for w in cast_weights]
    assert all(w.shape[1] == r * n and r % BF16_ROWS == 0 for w, r in zip(cast_weights, cast_rows))
    res = pl.pallas_call(
        functools.partial(_hgrn_kernel, n_cast=len(cast_weights)),
        grid=(n,),
        in_specs=[fwd(0), fwd(1), fwd(2), bwd(0), bwd(1), bwd(3),
                  pl.BlockSpec((None, 2, HG_W), lambda i: (layer, 0, 0)),
                  pl.BlockSpec((2, HG_SEL_ROWS, 2 * CHUNK), lambda i: (0, 0, 0)),
                  pl.BlockSpec((t, NA_W), lambda i: (i, cb_nk)),
                  kg_spec]
                 + [pl.BlockSpec((None, r, w.shape[2]), lambda i: (layer, i, 0))
                    for w, r in zip(cast_weights, cast_rows)],
        out_specs=[pl.BlockSpec((t, HG_W), lambda i: (i, 0)),
                   pl.BlockSpec((t, HG_W), lambda i: (n - 1 - i, 0)),
                   pl.BlockSpec((NA_HEADS, 1, NA_HD, t), lambda i: (0, i // per_tile, 0, i % per_tile))]
                  + [pl.BlockSpec((r, w.shape[2]), lambda i: (i, 0))
                     for w, r in zip(cast_weights, cast_rows)],
        out_shape=[jax.ShapeDtypeStruct((s, HG_W), BF16)] * 2
                  + [jax.ShapeDtypeStruct((NA_HEADS, s // NA_KT_TOKENS, NA_HD, NA_KT_TOKENS), BF16)]
                  + [jax.ShapeDtypeStruct(w.shape[1:], BF16) for w in cast_weights],
        scratch_shapes=[pltpu.VMEM((heads, HG_DK, HG_DK), F32)] * 2,
        compiler_params=pltpu.CompilerParams(
            dimension_semantics=("arbitrary",),
            vmem_limit_bytes=VMEM_LIMIT),
        name="hgrn",
    )(u, u, u, u, u, u, lb, jnp.asarray(_hgrn_selectors(), BF16), u, kg, *cast_weights)
    return res[0], res[1], res[2], list(res[3:])


NA_NEG_BLOCK = 2 * NA_KH - 1


def _natten_fill_bias(tz_ref, bias_ref, w0, r0, rows):
    kh = min(NA_KH, rows)
    lane = lax.broadcasted_iota(jnp.int32, (GRID_W, 2 * GRID_W), 1)

    def block_index(qi, a):
        r = r0 + qi
        r_start = min(max(r - kh // 2, 0), rows - kh)
        ar = w0 + a
        if r_start <= ar < r_start + kh:
            return ar - r + (NA_KH - 1)
        return NA_NEG_BLOCK

    for qi in range(NA_ROWS_PER_STEP):
        for ap in range(NA_WIN_ROWS // 2):
            ie, io = block_index(qi, 2 * ap), block_index(qi, 2 * ap + 1)
            blk = tz_ref[0, ie]
            if io != ie:
                blk = jnp.where(lane < GRID_W, blk, tz_ref[0, io])
            bias_ref[qi * GRID_W:(qi + 1) * GRID_W, ap * 2 * GRID_W:(ap + 1) * 2 * GRID_W] = blk


def _natten_kernel(q_ref, knt_ref, v_ref, gate_ref, qg_ref, tz_ref, o_ref, bias_ref):
    step = pl.program_id(1)
    s = v_ref.shape[0]
    rows = s // GRID_W
    rq, rk = NA_ROWS_PER_STEP, NA_WIN_ROWS
    n_blocks = rows // rq
    tq, win = rq * GRID_W, rk * GRID_W
    win_tiles = win // NA_KT_TOKENS

    @pl.when(step == 0)
    def _():
        _natten_fill_bias(tz_ref, bias_ref.at[0], 0, 0, rows)
        _natten_fill_bias(tz_ref, bias_ref.at[1], max(rq - NA_KH // 2, 0), rq, rows)
        _natten_fill_bias(tz_ref, bias_ref.at[2], rows - rk, rows - rq, rows)

    q = q_ref[...].astype(F32)
    ms = jnp.mean(q * q, axis=-1, keepdims=True)
    qn = (q * lax.rsqrt(ms + EPS) * (qg_ref[...] * (NA_HD ** -0.5 * LOG2E))).astype(BF16)

    qs, kws, vws, variants = [], [], [], []
    for c in range(NA_CHAINS):
        rb = step * NA_CHAINS + c
        w0 = jnp.clip(rq * rb - NA_KH // 2, 0, rows - rk) * GRID_W
        w0 = pl.multiple_of(w0, GRID_W)
        qs.append(qn[c * tq:(c + 1) * tq])
        kt = knt_ref[pl.ds(w0 // NA_KT_TOKENS, win_tiles)]
        kws.append(jnp.concatenate([kt[i] for i in range(win_tiles)], axis=1))
        vws.append(v_ref[pl.ds(w0, win), :])
        variants.append(jnp.where(rb == 0, 0, jnp.where(rb == n_blocks - 1, 2, 1)))
    scs = [_dot(qc, kw) for qc, kw in zip(qs, kws)]
    scs = [sc + bias_ref[var] for sc, var in zip(scs, variants)]
    ms = [jnp.max(sc, axis=-1, keepdims=True) for sc in scs]
    ps = [jnp.exp2(sc - m) for sc, m in zip(scs, ms)]
    ls = [jnp.sum(p, axis=-1, keepdims=True) for p in ps]
    outs = [_dot(p.astype(BF16), vw) for p, vw in zip(ps, vws)]
    for c, (o, l) in enumerate(zip(outs, ls)):
        gate = _silu(gate_ref[c * tq:(c + 1) * tq, :].astype(F32))
        o_ref[c * tq:(c + 1) * tq, :] = (o / l * gate).astype(o_ref.dtype)


def _natten_toeplitz(rpb):
    c = np.arange(GRID_W)
    c_start = np.clip(c - NA_KW // 2, 0, GRID_W - NA_KW)
    col_in = (c[None, :] >= c_start[:, None]) & (c[None, :] < c_start[:, None] + NA_KW)
    col_off = np.clip(c[None, :] - c[:, None] + (NA_KW - 1), 0, 2 * NA_KW - 2)
    sel = ((col_off[..., None] == np.arange(2 * NA_KW - 1)) & col_in[..., None]).astype(np.float32)
    sel = np.concatenate([sel, sel], axis=1) * LOG2E
    valid = np.concatenate([col_in, col_in], axis=1)[None] & (np.arange(2 * NA_KH) < NA_NEG_BLOCK)[:, None, None]
    neg = np.where(valid, 0.0, NEG).astype(np.float32)
    rpb = jnp.pad(rpb.astype(F32), ((0, 0), (0, 0), (0, 1), (0, 0)))
    return jnp.einsum('lhij,ckj->lhick', rpb, sel, precision=lax.Precision.HIGHEST) + neg


def _natten(u, knt, qg, tz, layer, col0):
    s = u.shape[0]
    blk = NA_ROWS_PER_STEP * GRID_W
    tq = NA_CHAINS * blk
    n_steps = s // tq
    assert s % tq == 0 and s // blk >= 3 and s // GRID_W >= NA_WIN_ROWS
    qg_spec, qg = _layer_row(qg, layer)
    assert blk % NA_KT_TOKENS == 0 and (NA_KH // 2 * GRID_W) % NA_KT_TOKENS == 0
    assert (NA_WIN_ROWS * GRID_W) % NA_KT_TOKENS == 0
    return pl.pallas_call(
        _natten_kernel,
        grid=(NA_HEADS, n_steps),
        in_specs=[
            pl.BlockSpec((tq, NA_HD), lambda h, rb: (rb, col0 + h)),
            pl.BlockSpec((None,) + knt.shape[1:], lambda h, rb: (h, 0, 0, 0)),
            pl.BlockSpec((s, NA_HD), lambda h, rb: (0, col0 + 2 * NA_HEADS + h)),
            pl.BlockSpec((tq, NA_HD), lambda h, rb: (rb, col0 + 3 * NA_HEADS + h)),
            qg_spec,
            pl.BlockSpec((None, 1) + tz.shape[2:], lambda h, rb: (layer, h, 0, 0, 0)),
        ],
        out_specs=pl.BlockSpec((tq, NA_HD), lambda h, rb: (rb, h)),
        out_shape=jax.ShapeDtypeStruct((s, NA_W), BF16),
        scratch_shapes=[pltpu.VMEM((3, blk, NA_WIN_ROWS * GRID_W), F32)],
        compiler_params=pltpu.CompilerParams(
            dimension_semantics=("parallel", "arbitrary"),
            vmem_limit_bytes=VMEM_LIMIT),
        name="natten",
    )(u, knt, u, u, qg, tz)


def _merge_kernel(of_ref, ob_ref, hg_ref, bin_ref, ma0_ref, ma1_ref, mb0_ref, mb1_ref,
                  x_ref, p_ref, on_ref, pn_ref, gn_ref, wa_ref, wb_ref, wo_ref, wg_ref, wp_ref,
                  o_ref, *maybe_hn_ref):
    oa = of_ref[...].astype(F32) + ob_ref[...].astype(F32)
    gate_a = _silu(hg_ref[...].astype(F32))
    parts = []
    for h in range(HG_HEADS):
        hs = slice(h * HG_DK, (h + 1) * HG_DK)
        oh = oa[:, hs]
        ms = jnp.mean(oh * oh, axis=-1, keepdims=True)
        parts.append((oh * lax.rsqrt(ms + EPS) * on_ref[...] * gate_a[:, hs]).astype(BF16))
    a_in = jnp.concatenate(parts, axis=1)
    b_in = bin_ref[...]
    y_parts = []
    for t in range(0, wo_ref.shape[0], MERGE_TN):
        ya = _dot(a_in, wa_ref[:, t:t + MERGE_TN])
        yb = _dot(b_in, wb_ref[:, t:t + MERGE_TN])
        ma_ref = ma0_ref if t < HG_W else ma1_ref
        mb_ref = mb0_ref if t < HG_W else mb1_ref
        tt = t % HG_W
        m_a = ma_ref[:, tt:tt + MERGE_TN].astype(F32)
        m_b = mb_ref[:, tt:tt + MERGE_TN].astype(F32)
        y_parts.append((_sigmoid(m_a) * ya + _sigmoid(m_b) * yb).astype(BF16))
    x1 = x_ref[...] + _dot(jnp.concatenate(y_parts, axis=1), wo_ref[...])
    inv = lax.rsqrt(jnp.mean(x1 * x1, axis=-1, keepdims=True) + EPS)
    xg = (x1 * pn_ref[...]).astype(BF16)
    pb = p_ref[...].astype(BF16)
    x2_parts = []
    for t in range(0, wg_ref.shape[1], MERGE_TN):
        g = _sigmoid(_dot(xg, wg_ref[:, t:t + MERGE_TN]) * inv)
        x2_parts.append(x1[:, t:t + MERGE_TN] + _dot(pb, wp_ref[:, t:t + MERGE_TN]) * g)
    x2 = jnp.concatenate(x2_parts, axis=1)
    o_ref[...] = x2
    for hn_ref in maybe_hn_ref:
        hn_ref[...] = _rms_bf16(x2, gn_ref[...])


def _merge(o_f, o_b, u, b_in, x, p, p_index, onorm, pnorm, in_norm, layer, weights, cb_hgate, cb_ma,
           tm=MERGE_TM):
    s, d = x.shape
    tm = min(tm, s)
    pd = p.shape[-1]
    act = lambda wdt: pl.BlockSpec((tm, wdt), lambda i: (i, 0))
    ucol = lambda cb: pl.BlockSpec((tm, HG_W), lambda i, cb=cb: (i, cb))
    const = lambda shp: pl.BlockSpec(shp, lambda i: (0, 0), pipeline_mode=pl.Buffered(1))
    emit_next = layer + 1 < in_norm.shape[0]
    on_spec, onorm = _layer_row(onorm, layer)
    pn_spec, pnorm = _layer_row(pnorm, layer)
    gn_spec, in_norm = _layer_row(in_norm, layer + 1 if emit_next else layer)
    out_specs = [act(d)] + ([act(d)] if emit_next else [])
    out_shape = [jax.ShapeDtypeStruct((s, d), F32)] + (
        [jax.ShapeDtypeStruct((s, d), BF16)] if emit_next else [])
    res = pl.pallas_call(
        _merge_kernel,
        grid=(s // tm,),
        in_specs=[act(HG_W), act(HG_W), ucol(cb_hgate), act(NA_W),
                  ucol(cb_ma), ucol(cb_ma + 1), ucol(cb_ma + 2), ucol(cb_ma + 3),
                  act(d), pl.BlockSpec((None, tm, pd), lambda i: (p_index, i, 0)),
                  on_spec, pn_spec, gn_spec]
                 + [const(w.shape) for w in weights],
        out_specs=out_specs,
        out_shape=out_shape,
        compiler_params=pltpu.CompilerParams(
            dimension_semantics=("parallel",),
            vmem_limit_bytes=VMEM_LIMIT),
        name="merge",
    )(o_f, o_b, u, b_in, u, u, u, u, x, p, onorm, pnorm, in_norm, *weights)
    return (res[0], res[1]) if emit_next else (res[0], None)


def kernel(x, p, norm_g, w_in, hgrn_lb, hgrn_onorm, na_qnorm, na_knorm, na_rpb,
           w_branch_a, w_branch_b, w_out, ple_norm, w_ple_gate, w_ple):
    bsz, s, d = x.shape
    depth = w_in.shape[0]
    lbp = jax.nn.softmax(hgrn_lb.astype(F32), axis=0)
    lower = jnp.cumsum(lbp, axis=0) - lbp[0:1]
    p_flat = p.reshape(depth * bsz, s, p.shape[-1])
    tz = _natten_toeplitz(na_rpb)
    w_ple_bf16 = w_ple.astype(BF16)
    cb_hgate = 4 * HG_W // HG_W
    cb_na = 5 * HG_W // NA_HD
    cb_ma = (5 * HG_W + 4 * NA_W) // HG_W

    outs = []
    for bi in range(bsz):
        xb = x[bi]
        h = _rmsnorm(xb, norm_g, 0)
        for i in range(depth):
            u = _in_proj(h, w_in, i)
            o_f, o_b, knt, weights = _hgrn(u, lower, na_knorm, (5 * HG_W + NA_W) // NA_W,
                                           (w_branch_a, w_branch_b, w_out, w_ple_gate), i)
            b_in = _natten(u, knt, na_qnorm, tz, i, col0=cb_na)
            xb, h = _merge(o_f, o_b, u, b_in, xb, p_flat, i * bsz + bi, hgrn_onorm, ple_norm, norm_g, i,
                           weights + [w_ple_bf16[i]], cb_hgate=cb_hgate, cb_ma=cb_ma)
        outs.append(xb)
    return jnp.stack(outs, axis=0)
```

```python
import functools

import jax
import jax.numpy as jnp
import numpy as np
from jax import lax
from jax.experimental import pallas as pl
from jax.experimental.pallas import tpu as pltpu

F32 = jnp.float32
BF16 = jnp.bfloat16

GRID_W = 64
HG_HEADS = 8
HG_DK = 128
HG_W = HG_HEADS * HG_DK
NA_HEADS = 8
NA_HD = 128
NA_W = NA_HEADS * NA_HD
NA_KH = 8
NA_KW = 16
EPS = 1e-6
NEG = -1e30
F_MIN = 1e-6

CHUNK = 64
HG_CHUNKS_PER_STEP = 4
SUB = 8
NSUB = CHUNK // SUB
MID = SUB // 2

NA_ROWS_PER_STEP = 4
NA_WIN_ROWS = 12
NA_CHAINS = 16
NA_KT_TOKENS = 256
LOG2E = 1.4426950408889634

VMEM_LIMIT = 56 * 1024 * 1024
BF16_ROWS = 16

RMSNORM_TM = 1024
IN_PROJ_TM = 2048
IN_PROJ_TN = 1024
MERGE_TM = 256
MERGE_TN = 512


def _dot(a, b):
    return jnp.dot(a, b, preferred_element_type=F32)


def _dot_nt(a, b):
    return lax.dot_general(a, b, (((1,), (1,)), ((), ())), preferred_element_type=F32)


def _sigmoid(x):
    return 1.0 / (1.0 + jnp.exp(-x))


def _silu(x):
    return x * _sigmoid(x)


def _layer_row(stack, layer):
    n = stack.shape[-1]
    return pl.BlockSpec((None, 1, n), lambda *_: (layer, 0, 0)), stack.reshape(stack.shape[0], 1, n)


def _rms_bf16(x, g):
    ms = jnp.mean(x * x, axis=-1, keepdims=True)
    return (x * lax.rsqrt(ms + EPS) * g).astype(BF16)


def _rmsnorm_kernel(x_ref, g_ref, o_ref):
    o_ref[...] = _rms_bf16(x_ref[...], g_ref[...])


def _rmsnorm(x, gains, layer, tm=RMSNORM_TM):
    s, d = x.shape
    tm = min(tm, s)
    g_spec, g = _layer_row(gains, layer)
    return pl.pallas_call(
        _rmsnorm_kernel,
        grid=(s // tm,),
        in_specs=[pl.BlockSpec((tm, d), lambda i: (i, 0)), g_spec],
        out_specs=pl.BlockSpec((tm, d), lambda i: (i, 0)),
        out_shape=jax.ShapeDtypeStruct((s, d), BF16),
        compiler_params=pltpu.CompilerParams(
            dimension_semantics=("parallel",),
            vmem_limit_bytes=VMEM_LIMIT),
        name="rmsnorm",
    )(x, g)


def _inproj_kernel(h_ref, w_ref, o_ref, wb_ref):
    @pl.when(pl.program_id(1) == 0)
    def _():
        wb_ref[...] = w_ref[...].astype(BF16)

    o_ref[...] = _dot(h_ref[...], wb_ref[...]).astype(o_ref.dtype)


def _in_proj(h, w, layer, tm=IN_PROJ_TM, tn=IN_PROJ_TN):
    s, d = h.shape
    n = w.shape[2]
    tm = min(tm, s)
    return pl.pallas_call(
        _inproj_kernel,
        grid=(n // tn, s // tm),
        in_specs=[
            pl.BlockSpec((tm, d), lambda j, i: (i, 0)),
            pl.BlockSpec((None, d, tn), lambda j, i: (layer, 0, j)),
        ],
        out_specs=pl.BlockSpec((tm, tn), lambda j, i: (i, j)),
        out_shape=jax.ShapeDtypeStruct((s, n), BF16),
        scratch_shapes=[pltpu.VMEM((d, tn), BF16)],
        compiler_params=pltpu.CompilerParams(
            dimension_semantics=("parallel", "arbitrary"),
            vmem_limit_bytes=VMEM_LIMIT),
        name="in_proj",
    )(h, w)


def _hgrn_selectors():
    c = CHUNK
    t = np.arange(c)[:, None]
    r = np.arange(c)[None, :]
    m = (t // SUB) * SUB + MID
    out = []
    for backward in (False, True):
        if backward:
            cum = lambda p: (r >= p).astype(np.float32)
            prev_m = m - SUB
        else:
            cum = lambda p: (r <= p).astype(np.float32)
            prev_m = m + SUB
        has_prev = (prev_m >= 0) & (prev_m < c)
        step = np.where(has_prev, cum(np.clip(prev_m, 0, c - 1)) - cum(m), 0.0)
        total = np.ones((BF16_ROWS, c), np.float32)
        first_m, last_m = (MID, c - SUB + MID) if backward else (c - SUB + MID, MID)
        at_last = np.broadcast_to(cum(last_m), (BF16_ROWS, c))
        past_first = np.broadcast_to(1.0 - cum(first_m), (BF16_ROWS, c))
        sel = np.concatenate([cum(t) - cum(m), step, at_last, past_first, total], axis=0)
        out.append(np.concatenate([sel, sel], axis=1))
    return np.stack(out)


HG_SEL_ROWS = 2 * CHUNK + 3 * BF16_ROWS


def _hgrn_prologue(q_ref, v_ref, z_ref, rows, lb, sel, backward):
    c = CHUNK
    w = q_ref.shape[1]
    q = q_ref[rows, :].astype(F32)
    v = v_ref[rows, :]
    z = z_ref[rows, :].astype(F32)

    f = lb + (1.0 - lb) * _sigmoid(z)
    g = jnp.log2(jnp.maximum(f, F_MIN))
    k = 1.0 - f
    qs = q * _sigmoid(q)

    hi = g.astype(BF16)
    lo = (g - hi.astype(F32)).astype(BF16)
    cums = _dot(sel, jnp.concatenate([hi, lo], axis=0))
    e = jnp.exp2(cums[0:c])
    step = jnp.exp2(cums[c:2 * c])
    tail = [jnp.exp2(cums[2 * c + i * BF16_ROWS:2 * c + i * BF16_ROWS + SUB]) for i in range(3)]
    last_to_state, end_to_first, dec = tail

    q0 = qs * e
    k0 = k * (1.0 / e)

    blocks = [slice(j * SUB, (j + 1) * SUB) for j in range(NSUB)]
    zero_blk = jnp.zeros((SUB, w), F32)
    order = range(NSUB) if backward else range(NSUB - 1, -1, -1)
    cur = [None] * NSUB
    q_tilde = [None] * NSUB
    k_tilde = [None] * NSUB
    k_hat_blocks = [None] * NSUB
    from_ref = None
    for j in order:
        for jj in range(NSUB):
            if cur[jj] is not None:
                cur[jj] = cur[jj] * step[blocks[j]]
        cur[j] = q0[blocks[j]]
        from_ref = end_to_first if from_ref is None else from_ref * step[blocks[j]]
        k_hat_blocks[j] = k0[blocks[j]] * from_ref
        q_tilde[j] = jnp.concatenate(
            [zero_blk if cur[jj] is None else cur[jj] for jj in range(NSUB)], axis=0).astype(BF16)
        k_tilde[j] = jnp.concatenate(
            [k0[blocks[jj]] if jj == j else zero_blk for jj in range(NSUB)], axis=0).astype(BF16)

    q_hat = jnp.concatenate([blk * last_to_state for blk in cur], axis=0).astype(BF16)
    k_hat = jnp.concatenate(k_hat_blocks, axis=0).astype(BF16)

    row = lax.broadcasted_iota(jnp.int32, (c, c), 0)
    col = lax.broadcasted_iota(jnp.int32, (c, c), 1)
    causal = (row >= col) if backward else (row <= col)
    heads = []
    for h in range(w // HG_DK):
        hs = slice(h * HG_DK, (h + 1) * HG_DK)
        heads.append(dict(
            q_hat=q_hat[:, hs], k_hat=k_hat[:, hs], v=v[:, hs], dec=dec[:, hs], backward=backward,
            q_cat=jnp.concatenate([q_tilde[j][:, hs] for j in range(NSUB)], axis=1),
            k_cat=jnp.concatenate([k_tilde[j][:, hs] for j in range(NSUB)], axis=1)))
    return heads, causal


def _hgrn_scores_t(q_cat, k_cat, backward):
    rt = 2 * SUB
    kw = 2 * HG_DK
    rows = []
    for p in range(CHUNK // rt):
        t_rows = slice(0, (p + 1) * rt) if backward else slice(p * rt, CHUNK)
        part = _dot_nt(k_cat[p * rt:(p + 1) * rt, p * kw:(p + 1) * kw], q_cat[t_rows, p * kw:(p + 1) * kw])
        pad = CHUNK - part.shape[1]
        if pad:
            zero = jnp.zeros((rt, pad), F32)
            part = jnp.concatenate([part, zero] if backward else [zero, part], axis=1)
        rows.append(part)
    return jnp.concatenate(rows, axis=0)


def _hgrn_kernel(*refs, n_cast):
    (qf_ref, vf_ref, zf_ref, qb_ref, vb_ref, zb_ref, lb_ref, sel_ref, nk_ref, kg_ref) = refs[:10]
    refs = refs[10:]
    w32_refs, refs = refs[:n_cast], refs[n_cast:]
    (of_ref, ob_ref, knt_ref), refs = refs[:3], refs[3:]
    w16_refs, (stf_ref, stb_ref) = refs[:n_cast], refs[n_cast:]

    @pl.when(pl.program_id(0) == 0)
    def _():
        stf_ref[...] = jnp.zeros_like(stf_ref)
        stb_ref[...] = jnp.zeros_like(stb_ref)

    for w32_ref, w16_ref in zip(w32_refs, w16_refs):
        w16_ref[...] = w32_ref[...].astype(w16_ref.dtype)
    for h in range(knt_ref.shape[0]):
        kk = nk_ref[:, h * NA_HD:(h + 1) * NA_HD].astype(F32)
        knt_ref[h, 0] = _rms_bf16(kk, kg_ref[...]).T

    n_heads = stf_ref.shape[0]
    n_chunks = qf_ref.shape[0] // CHUNK
    rows_f = [slice(c * CHUNK, (c + 1) * CHUNK) for c in range(n_chunks)]
    rows_b = rows_f[::-1]
    pro = [(_hgrn_prologue(qf_ref, vf_ref, zf_ref, rf, lb_ref[0:1, :], sel_ref[0], False),
            _hgrn_prologue(qb_ref, vb_ref, zb_ref, rb, lb_ref[1:2, :], sel_ref[1], True))
           for rf, rb in zip(rows_f, rows_b)]

    states = [stf_ref[h] for h in range(n_heads)] + [stb_ref[h] for h in range(n_heads)]
    for (rf, rb), ((heads_f, causal_f), (heads_b, causal_b)) in zip(zip(rows_f, rows_b), pro):
        work = ([(heads_f[h], causal_f, of_ref, rf, h) for h in range(n_heads)]
                + [(heads_b[h], causal_b, ob_ref, rb, h) for h in range(n_heads)])
        atts = [_hgrn_scores_t(a["q_cat"], a["k_cat"], a["backward"]) for (a, _, _, _, _) in work]
        updates = [_dot(a["v"].T, a["k_hat"]) for (a, _, _, _, _) in work]
        atts = [jnp.where(causal, att, 0.0).astype(BF16).T
                for att, (_, causal, _, _, _) in zip(atts, work)]
        outs = [_dot(jnp.concatenate([a["q_hat"], att], axis=1),
                     jnp.concatenate([st.astype(BF16).T, a["v"]], axis=0))
                for att, st, (a, _, _, _, _) in zip(atts, states, work)]
        new_states = []
        for (a, _, o_ref, rows, h), st, out, upd in zip(work, states, outs, updates):
            o_ref[rows, h * HG_DK:(h + 1) * HG_DK] = out.astype(o_ref.dtype)
            decayed = st.reshape(HG_DK // SUB, SUB, HG_DK) * a["dec"][None]
            new_states.append(decayed.reshape(HG_DK, HG_DK) + upd)
        states = new_states
    for h in range(n_heads):
        stf_ref[h] = states[h]
        stb_ref[h] = states[n_heads + h]


def _hgrn(u, lb, kg, cb_nk, cast_weights=(), layer=0):
    s = u.shape[0]
    t = HG_CHUNKS_PER_STEP * CHUNK
    n = s // t
    assert s % t == 0 and NA_KT_TOKENS % t == 0 and t % 128 == 0
    per_tile = NA_KT_TOKENS // t
    fwd = lambda cb: pl.BlockSpec((t, HG_W), lambda i, cb=cb: (i, cb))
    bwd = lambda cb: pl.BlockSpec((t, HG_W), lambda i, cb=cb: (n - 1 - i, cb))
    heads = HG_W // HG_DK
    kg_spec, kg = _layer_row(kg, layer)
    cast_rows = [w.shape[1] // n for w in cast_weights]
    assert all(w.shape[1] == r * n and r % BF16_ROWS == 0 for w, r in zip(cast_weights, cast_rows))
    res = pl.pallas_call(
        functools.partial(_hgrn_kernel, n_cast=len(cast_weights)),
        grid=(n,),
        in_specs=[fwd(0), fwd(1), fwd(2), bwd(0), bwd(1), bwd(3),
                  pl.BlockSpec((None, 2, HG_W), lambda i: (layer, 0, 0)),
                  pl.BlockSpec((2, HG_SEL_ROWS, 2 * CHUNK), lambda i: (0, 0, 0)),
                  pl.BlockSpec((t, NA_W), lambda i: (i, cb_nk)),
                  kg_spec]
                 + [pl.BlockSpec((None, r, w.shape[2]), lambda i: (layer, i, 0))
                    for w, r in zip(cast_weights, cast_rows)],
        out_specs=[pl.BlockSpec((t, HG_W), lambda i: (i, 0)),
                   pl.BlockSpec((t, HG_W), lambda i: (n - 1 - i, 0)),
                   pl.BlockSpec((NA_HEADS, 1, NA_HD, t), lambda i: (0, i // per_tile, 0, i % per_tile))]
                  + [pl.BlockSpec((r, w.shape[2]), lambda i: (i, 0))
                     for w, r in zip(cast_weights, cast_rows)],
        out_shape=[jax.ShapeDtypeStruct((s, HG_W), BF16)] * 2
                  + [jax.ShapeDtypeStruct((NA_HEADS, s // NA_KT_TOKENS, NA_HD, NA_KT_TOKENS), BF16)]
                  + [jax.ShapeDtypeStruct(w.shape[1:], BF16) for w in cast_weights],
        scratch_shapes=[pltpu.VMEM((heads, HG_DK, HG_DK), F32)] * 2,
        compiler_params=pltpu.CompilerParams(
            dimension_semantics=("arbitrary",),
            vmem_limit_bytes=VMEM_LIMIT),
        name="hgrn",
    )(u, u, u, u, u, u, lb, jnp.asarray(_hgrn_selectors(), BF16), u, kg, *cast_weights)
    return res[0], res[1], res[2], list(res[3:])


NA_NEG_BLOCK = 2 * NA_KH - 1


def _natten_fill_bias(tz_ref, bias_ref, w0, r0, rows):
    kh = min(NA_KH, rows)
    lane = lax.broadcasted_iota(jnp.int32, (GRID_W, 2 * GRID_W), 1)

    def block_index(qi, a):
        r = r0 + qi
        r_start = min(max(r - kh // 2, 0), rows - kh)
        ar = w0 + a
        if r_start <= ar < r_start + kh:
            return ar - r + (NA_KH - 1)
        return NA_NEG_BLOCK

    for qi in range(NA_ROWS_PER_STEP):
        for ap in range(NA_WIN_ROWS // 2):
            ie, io = block_index(qi, 2 * ap), block_index(qi, 2 * ap + 1)
            blk = tz_ref[0, ie]
            if io != ie:
                blk = jnp.where(lane < GRID_W, blk, tz_ref[0, io])
            bias_ref[qi * GRID_W:(qi + 1) * GRID_W, ap * 2 * GRID_W:(ap + 1) * 2 * GRID_W] = blk


def _natten_kernel(q_ref, knt_ref, v_ref, gate_ref, qg_ref, tz_ref, o_ref, bias_ref):
    step = pl.program_id(1)
    s = v_ref.shape[0]
    rows = s // GRID_W
    rq, rk = NA_ROWS_PER_STEP, NA_WIN_ROWS
    n_blocks = rows // rq
    tq, win = rq * GRID_W, rk * GRID_W
    win_tiles = win // NA_KT_TOKENS

    @pl.when(step == 0)
    def _():
        _natten_fill_bias(tz_ref, bias_ref.at[0], 0, 0, rows)
        _natten_fill_bias(tz_ref, bias_ref.at[1], max(rq - NA_KH // 2, 0), rq, rows)
        _natten_fill_bias(tz_ref, bias_ref.at[2], rows - rk, rows - rq, rows)

    q = q_ref[...].astype(F32)
    ms = jnp.mean(q * q, axis=-1, keepdims=True)
    qn = (q * lax.rsqrt(ms + EPS) * (qg_ref[...] * (NA_HD ** -0.5 * LOG2E))).astype(BF16)

    qs, kws, vws, variants = [], [], [], []
    for c in range(NA_CHAINS):
        rb = step * NA_CHAINS + c
        w0 = jnp.clip(rq * rb - NA_KH // 2, 0, rows - rk) * GRID_W
        w0 = pl.multiple_of(w0, GRID_W)
        qs.append(qn[c * tq:(c + 1) * tq])
        kt = knt_ref[pl.ds(w0 // NA_KT_TOKENS, win_tiles)]
        kws.append(jnp.concatenate([kt[i] for i in range(win_tiles)], axis=1))
        vws.append(v_ref[pl.ds(w0, win), :])
        variants.append(jnp.where(rb == 0, 0, jnp.where(rb == n_blocks - 1, 2, 1)))
    scs = [_dot(qc, kw) for qc, kw in zip(qs, kws)]
    scs = [sc + bias_ref[var] for sc, var in zip(scs, variants)]
    ms = [jnp.max(sc, axis=-1, keepdims=True) for sc in scs]
    ps = [jnp.exp2(sc - m) for sc, m in zip(scs, ms)]
    ls = [jnp.sum(p, axis=-1, keepdims=True) for p in ps]
    outs = [_dot(p.astype(BF16), vw) for p, vw in zip(ps, vws)]
    for c, (o, l) in enumerate(zip(outs, ls)):
        gate = _silu(gate_ref[c * tq:(c + 1) * tq, :].astype(F32))
        o_ref[c * tq:(c + 1) * tq, :] = (o / l * gate).astype(o_ref.dtype)


def _natten_toeplitz(rpb):
    c = np.arange(GRID_W)
    c_start = np.clip(c - NA_KW // 2, 0, GRID_W - NA_KW)
    col_in = (c[None, :] >= c_start[:, None]) & (c[None, :] < c_start[:, None] + NA_KW)
    col_off = np.clip(c[None, :] - c[:, None] + (NA_KW - 1), 0, 2 * NA_KW - 2)
    sel = ((col_off[..., None] == np.arange(2 * NA_KW - 1)) & col_in[..., None]).astype(np.float32)
    sel = np.concatenate([sel, sel], axis=1) * LOG2E
    valid = np.concatenate([col_in, col_in], axis=1)[None] & (np.arange(2 * NA_KH) < NA_NEG_BLOCK)[:, None, None]
    neg = np.where(valid, 0.0, NEG).astype(np.float32)
    rpb = jnp.pad(rpb.astype(F32), ((0, 0), (0, 0), (0, 1), (0, 0)))
    return jnp.einsum('lhij,ckj->lhick', rpb, sel, precision=lax.Precision.HIGHEST) + neg


def _natten(u, knt, qg, tz, layer, col0):
    s = u.shape[0]
    blk = NA_ROWS_PER_STEP * GRID_W
    tq = NA_CHAINS * blk
    n_steps = s // tq
    assert s % tq == 0 and s // blk >= 3 and s // GRID_W >= NA_WIN_ROWS
    qg_spec, qg = _layer_row(qg, layer)
    assert blk % NA_KT_TOKENS == 0 and (NA_KH // 2 * GRID_W) % NA_KT_TOKENS == 0
    assert (NA_WIN_ROWS * GRID_W) % NA_KT_TOKENS == 0
    return pl.pallas_call(
        _natten_kernel,
        grid=(NA_HEADS, n_steps),
        in_specs=[
            pl.BlockSpec((tq, NA_HD), lambda h, rb: (rb, col0 + h)),
            pl.BlockSpec((None,) + knt.shape[1:], lambda h, rb: (h, 0, 0, 0)),
            pl.BlockSpec((s, NA_HD), lambda h, rb: (0, col0 + 2 * NA_HEADS + h)),
            pl.BlockSpec((tq, NA_HD), lambda h, rb: (rb, col0 + 3 * NA_HEADS + h)),
            qg_spec,
            pl.BlockSpec((None, 1) + tz.shape[2:], lambda h, rb: (layer, h, 0, 0, 0)),
        ],
        out_specs=pl.BlockSpec((tq, NA_HD), lambda h, rb: (rb, h)),
        out_shape=jax.ShapeDtypeStruct((s, NA_W), BF16),
        scratch_shapes=[pltpu.VMEM((3, blk, NA_WIN_ROWS * GRID_W), F32)],
        compiler_params=pltpu.CompilerParams(
            dimension_semantics=("parallel", "arbitrary"),
            vmem_limit_bytes=VMEM_LIMIT),
        name="natten",
    )(u, knt, u, u, qg, tz)


def _merge_kernel(of_ref, ob_ref, hg_ref, bin_ref, ma0_ref, ma1_ref, mb0_ref, mb1_ref,
                  x_ref, p_ref, on_ref, pn_ref, gn_ref, wa_ref, wb_ref, wo_ref, wg_ref, wp_ref,
                  o_ref, *maybe_hn_ref):
    oa = of_ref[...].astype(F32) + ob_ref[...].astype(F32)
    gate_a = _silu(hg_ref[...].astype(F32))
    parts = []
    for h in range(HG_HEADS):
        hs = slice(h * HG_DK, (h + 1) * HG_DK)
        oh = oa[:, hs]
        ms = jnp.mean(oh * oh, axis=-1, keepdims=True)
        parts.append((oh * lax.rsqrt(ms + EPS) * on_ref[...] * gate_a[:, hs]).astype(BF16))
    a_in = jnp.concatenate(parts, axis=1)
    b_in = bin_ref[...]
    y_parts = []
    for t in range(0, wo_ref.shape[0], MERGE_TN):
        ya = _dot(a_in, wa_ref[:, t:t + MERGE_TN])
        yb = _dot(b_in, wb_ref[:, t:t + MERGE_TN])
        ma_ref = ma0_ref if t < HG_W else ma1_ref
        mb_ref = mb0_ref if t < HG_W else mb1_ref
        tt = t % HG_W
        m_a = ma_ref[:, tt:tt + MERGE_TN].astype(F32)
        m_b = mb_ref[:, tt:tt + MERGE_TN].astype(F32)
        y_parts.append((_sigmoid(m_a) * ya + _sigmoid(m_b) * yb).astype(BF16))
    x1 = x_ref[...] + _dot(jnp.concatenate(y_parts, axis=1), wo_ref[...])
    inv = lax.rsqrt(jnp.mean(x1 * x1, axis=-1, keepdims=True) + EPS)
    xg = (x1 * pn_ref[...]).astype(BF16)
    pb = p_ref[...].astype(BF16)
    x2_parts = []
    for t in range(0, wg_ref.shape[1], MERGE_TN):
        g = _sigmoid(_dot(xg, wg_ref[:, t:t + MERGE_TN]) * inv)
        x2_parts.append(x1[:, t:t + MERGE_TN] + _dot(pb, wp_ref[:, t:t + MERGE_TN]) * g)
    x2 = jnp.concatenate(x2_parts, axis=1)
    o_ref[...] = x2
    for hn_ref in maybe_hn_ref:
        hn_ref[...] = _rms_bf16(x2, gn_ref[...])


def _merge(o_f, o_b, u, b_in, x, p, p_index, onorm, pnorm, in_norm, layer, weights, cb_hgate, cb_ma,
           tm=MERGE_TM):
    s, d = x.shape
    tm = min(tm, s)
    pd = p.shape[-1]
    act = lambda wdt: pl.BlockSpec((tm, wdt), lambda i: (i, 0))
    ucol = lambda cb: pl.BlockSpec((tm, HG_W), lambda i, cb=cb: (i, cb))
    const = lambda shp: pl.BlockSpec(shp, lambda i: (0, 0), pipeline_mode=pl.Buffered(1))
    emit_next = layer + 1 < in_norm.shape[0]
    on_spec, onorm = _layer_row(onorm, layer)
    pn_spec, pnorm = _layer_row(pnorm, layer)
    gn_spec, in_norm = _layer_row(in_norm, layer + 1 if emit_next else layer)
    out_specs = [act(d)] + ([act(d)] if emit_next else [])
    out_shape = [jax.ShapeDtypeStruct((s, d), F32)] + (
        [jax.ShapeDtypeStruct((s, d), BF16)] if emit_next else [])
    res = pl.pallas_call(
        _merge_kernel,
        grid=(s // tm,),
        in_specs=[act(HG_W), act(HG_W), ucol(cb_hgate), act(NA_W),
                  ucol(cb_ma), ucol(cb_ma + 1), ucol(cb_ma + 2), ucol(cb_ma + 3),
                  act(d), pl.BlockSpec((None, tm, pd), lambda i: (p_index, i, 0)),
                  on_spec, pn_spec, gn_spec]
                 + [const(w.shape) for w in weights],
        out_specs=out_specs,
        out_shape=out_shape,
        compiler_params=pltpu.CompilerParams(
            dimension_semantics=("parallel",),
            vmem_limit_bytes=VMEM_LIMIT),
        name="merge",
    )(o_f, o_b, u, b_in, u, u, u, u, x, p, onorm, pnorm, in_norm, *weights)
    return (res[0], res[1]) if emit_next else (res[0], None)


def kernel(x, p, norm_g, w_in, hgrn_lb, hgrn_onorm, na_qnorm, na_knorm, na_rpb,
           w_branch_a, w_branch_b, w_out, ple_norm, w_ple_gate, w_ple):
    bsz, s, d = x.shape
    depth = w_in.shape[0]
    lbp = jax.nn.softmax(hgrn_lb.astype(F32), axis=0)
    lower = jnp.cumsum(lbp, axis=0) - lbp[0:1]
    p_flat = p.reshape(depth * bsz, s, p.shape[-1])
    tz = _natten_toeplitz(na_rpb)
    w_ple_bf16 = w_ple.astype(BF16)
    cb_hgate = 4 * HG_W // HG_W
    cb_na = 5 * HG_W // NA_HD
    cb_ma = (5 * HG_W + 4 * NA_W) // HG_W

    outs = []
    for bi in range(bsz):
        xb = x[bi]
        h = _rmsnorm(xb, norm_g, 0)
        for i in range(depth):
            u = _in_proj(h, w_in, i)
            o_f, o_b, knt, weights = _hgrn(u, lower, na_knorm, (5 * HG_W + NA_W) // NA_W,
                                           (w_branch_a, w_branch_b, w_out, w_ple_gate), i)
            b_in = _natten(u, knt, na_qnorm, tz, i, col0=cb_na)
            xb, h = _merge(o_f, o_b, u, b_in, xb, p_flat, i * bsz + bi, hgrn_onorm, ple_norm, norm_g, i,
                           weights + [w_ple_bf16[i]], cb_hgate=cb_hgate, cb_ma=cb_ma)
        outs.append(xb)
    return jnp.stack(outs, axis=0)
```

```python
import functools

import jax
import jax.numpy as jnp
import numpy as np
from jax import lax
from jax.experimental import pallas as pl
from jax.experimental.pallas import tpu as pltpu

F32 = jnp.float32
BF16 = jnp.bfloat16

GRID_W = 64
HG_HEADS = 8
HG_DK = 128
HG_W = HG_HEADS * HG_DK
NA_HEADS = 8
NA_HD = 128
NA_W = NA_HEADS * NA_HD
NA_KH = 8
NA_KW = 16
EPS = 1e-6
NEG = -1e30
F_MIN = 1e-6

CHUNK = 64
HG_CHUNKS_PER_STEP = 4
SUB = 8
NSUB = CHUNK // SUB
MID = SUB // 2

NA_ROWS_PER_STEP = 4
NA_WIN_ROWS = 12
NA_CHAINS = 16
NA_KT_TOKENS = 256
LOG2E = 1.4426950408889634

VMEM_LIMIT = 56 * 1024 * 1024
BF16_ROWS = 16

RMSNORM_TM = 1024
IN_PROJ_TM = 2048
IN_PROJ_TN = 1024
MERGE_TM = 256
MERGE_TN = 512


def _dot(a, b):
    return jnp.dot(a, b, preferred_element_type=F32)


def _dot_nt(a, b):
    return lax.dot_general(a, b, (((1,), (1,)), ((), ())), preferred_element_type=F32)


def _sigmoid(x):
    return 1.0 / (1.0 + jnp.exp(-x))


def _silu(x):
    return x * _sigmoid(x)


def _layer_row(stack, layer):
    n = stack.shape[-1]
    return pl.BlockSpec((None, 1, n), lambda *_: (layer, 0, 0)), stack.reshape(stack.shape[0], 1, n)


def _rms_bf16(x, g):
    ms = jnp.mean(x * x, axis=-1, keepdims=True)
    return (x * lax.rsqrt(ms + EPS) * g).astype(BF16)


def _rmsnorm_kernel(x_ref, g_ref, o_ref):
    o_ref[...] = _rms_bf16(x_ref[...], g_ref[...])


def _rmsnorm(x, gains, layer, tm=RMSNORM_TM):
    s, d = x.shape
    tm = min(tm, s)
    g_spec, g = _layer_row(gains, layer)
    return pl.pallas_call(
        _rmsnorm_kernel,
        grid=(s // tm,),
        in_specs=[pl.BlockSpec((tm, d), lambda i: (i, 0)), g_spec],
        out_specs=pl.BlockSpec((tm, d), lambda i: (i, 0)),
        out_shape=jax.ShapeDtypeStruct((s, d), BF16),
        compiler_params=pltpu.CompilerParams(
            dimension_semantics=("parallel",),
            vmem_limit_bytes=VMEM_LIMIT),
        name="rmsnorm",
    )(x, g)


def _inproj_kernel(h_ref, w_ref, o_ref, wb_ref):
    @pl.when(pl.program_id(1) == 0)
    def _():
        wb_ref[...] = w_ref[...].astype(BF16)

    o_ref[...] = _dot(h_ref[...], wb_ref[...]).astype(o_ref.dtype)


def _in_proj(h, w, layer, tm=IN_PROJ_TM, tn=IN_PROJ_TN):
    s, d = h.shape
    n = w.shape[2]
    tm = min(tm, s)
    return pl.pallas_call(
        _inproj_kernel,
        grid=(n // tn, s // tm),
        in_specs=[
            pl.BlockSpec((tm, d), lambda j, i: (i, 0)),
            pl.BlockSpec((None, d, tn), lambda j, i: (layer, 0, j)),
        ],
        out_specs=pl.BlockSpec((tm, tn), lambda j, i: (i, j)),
        out_shape=jax.ShapeDtypeStruct((s, n), BF16),
        scratch_shapes=[pltpu.VMEM((d, tn), BF16)],
        compiler_params=pltpu.CompilerParams(
            dimension_semantics=("parallel", "arbitrary"),
            vmem_limit_bytes=VMEM_LIMIT),
        name="in_proj",
    )(h, w)


def _hgrn_selectors():
    c = CHUNK
    t = np.arange(c)[:, None]
    r = np.arange(c)[None, :]
    m = (t // SUB) * SUB + MID
    out = []
    for backward in (False, True):
        if backward:
            cum = lambda p: (r >= p).astype(np.float32)
            prev_m = m - SUB
        else:
            cum = lambda p: (r <= p).astype(np.float32)
            prev_m = m + SUB
        has_prev = (prev_m >= 0) & (prev_m < c)
        step = np.where(has_prev, cum(np.clip(prev_m, 0, c - 1)) - cum(m), 0.0)
        total = np.ones((BF16_ROWS, c), np.float32)
        first_m, last_m = (MID, c - SUB + MID) if backward else (c - SUB + MID, MID)
        at_last = np.broadcast_to(cum(last_m), (BF16_ROWS, c))
        past_first = np.broadcast_to(1.0 - cum(first_m), (BF16_ROWS, c))
        sel = np.concatenate([cum(t) - cum(m), step, at_last, past_first, total], axis=0)
        out.append(np.concatenate([sel, sel], axis=1))
    return np.stack(out)


HG_SEL_ROWS = 2 * CHUNK + 3 * BF16_ROWS


def _hgrn_prologue(q_ref, v_ref, z_ref, rows, lb, sel, backward):
    c = CHUNK
    w = q_ref.shape[1]
    q = q_ref[rows, :].astype(F32)
    v = v_ref[rows, :]
    z = z_ref[rows, :].astype(F32)

    f = lb + (1.0 - lb) * _sigmoid(z)
    g = jnp.log2(jnp.maximum(f, F_MIN))
    k = 1.0 - f
    qs = q * _sigmoid(q)

    hi = g.astype(BF16)
    lo = (g - hi.astype(F32)).astype(BF16)
    cums = _dot(sel, jnp.concatenate([hi, lo], axis=0))
    e = jnp.exp2(cums[0:c])
    step = jnp.exp2(cums[c:2 * c])
    tail = [jnp.exp2(cums[2 * c + i * BF16_ROWS:2 * c + i * BF16_ROWS + SUB]) for i in range(3)]
    last_to_state, end_to_first, dec = tail

    q0 = qs * e
    k0 = k * (1.0 / e)

    blocks = [slice(j * SUB, (j + 1) * SUB) for j in range(NSUB)]
    zero_blk = jnp.zeros((SUB, w), F32)
    order = range(NSUB) if backward else range(NSUB - 1, -1, -1)
    cur = [None] * NSUB
    q_tilde = [None] * NSUB
    k_tilde = [None] * NSUB
    k_hat_blocks = [None] * NSUB
    from_ref = None
    for j in order:
        for jj in range(NSUB):
            if cur[jj] is not None:
                cur[jj] = cur[jj] * step[blocks[j]]
        cur[j] = q0[blocks[j]]
        from_ref = end_to_first if from_ref is None else from_ref * step[blocks[j]]
        k_hat_blocks[j] = k0[blocks[j]] * from_ref
        q_tilde[j] = jnp.concatenate(
            [zero_blk if cur[jj] is None else cur[jj] for jj in range(NSUB)], axis=0).astype(BF16)
        k_tilde[j] = jnp.concatenate(
            [k0[blocks[jj]] if jj == j else zero_blk for jj in range(NSUB)], axis=0).astype(BF16)

    q_hat = jnp.concatenate([blk * last_to_state for blk in cur], axis=0).astype(BF16)
    k_hat = jnp.concatenate(k_hat_blocks, axis=0).astype(BF16)

    row = lax.broadcasted_iota(jnp.int32, (c, c), 0)
    col = lax.broadcasted_iota(jnp.int32, (c, c), 1)
    causal = (row >= col) if backward else (row <= col)
    heads = []
    for h in range(w // HG_DK):
        hs = slice(h * HG_DK, (h + 1) * HG_DK)
        heads.append(dict(
            q_hat=q_hat[:, hs], k_hat=k_hat[:, hs], v=v[:, hs], dec=dec[:, hs], backward=backward,
            q_cat=jnp.concatenate([q_tilde[j][:, hs] for j in range(NSUB)], axis=1),
            k_cat=jnp.concatenate([k_tilde[j][:, hs] for j in range(NSUB)], axis=1)))
    return heads, causal


def _hgrn_scores_t(q_cat, k_cat, backward):
    rt = 2 * SUB
    kw = 2 * HG_DK
    rows = []
    for p in range(CHUNK // rt):
        t_rows = slice(0, (p + 1) * rt) if backward else slice(p * rt, CHUNK)
        part = _dot_nt(k_cat[p * rt:(p + 1) * rt, p * kw:(p + 1) * kw], q_cat[t_rows, p * kw:(p + 1) * kw])
        pad = CHUNK - part.shape[1]
        if pad:
            zero = jnp.zeros((rt, pad), F32)
            part = jnp.concatenate([part, zero] if backward else [zero, part], axis=1)
        rows.append(part)
    return jnp.concatenate(rows, axis=0)


def _hgrn_kernel(*refs, n_cast):
    (qf_ref, vf_ref, zf_ref, qb_ref, vb_ref, zb_ref, lb_ref, sel_ref, nk_ref, kg_ref) = refs[:10]
    refs = refs[10:]
    w32_refs, refs = refs[:n_cast], refs[n_cast:]
    (of_ref, ob_ref, knt_ref), refs = refs[:3], refs[3:]
    w16_refs, (stf_ref, stb_ref) = refs[:n_cast], refs[n_cast:]

    @pl.when(pl.program_id(0) == 0)
    def _():
        stf_ref[...] = jnp.zeros_like(stf_ref)
        stb_ref[...] = jnp.zeros_like(stb_ref)

    for w32_ref, w16_ref in zip(w32_refs, w16_refs):
        w16_ref[...] = w32_ref[...].astype(w16_ref.dtype)
    for h in range(knt_ref.shape[0]):
        kk = nk_ref[:, h * NA_HD:(h + 1) * NA_HD].astype(F32)
        knt_ref[h, 0] = _rms_bf16(kk, kg_ref[...]).T

    n_heads = stf_ref.shape[0]
    n_chunks = qf_ref.shape[0] // CHUNK
    rows_f = [slice(c * CHUNK, (c + 1) * CHUNK) for c in range(n_chunks)]
    rows_b = rows_f[::-1]
    pro = [(_hgrn_prologue(qf_ref, vf_ref, zf_ref, rf, lb_ref[0:1, :], sel_ref[0], False),
            _hgrn_prologue(qb_ref, vb_ref, zb_ref, rb, lb_ref[1:2, :], sel_ref[1], True))
           for rf, rb in zip(rows_f, rows_b)]

    states = [stf_ref[h] for h in range(n_heads)] + [stb_ref[h] for h in range(n_heads)]
    for (rf, rb), ((heads_f, causal_f), (heads_b, causal_b)) in zip(zip(rows_f, rows_b), pro):
        work = ([(heads_f[h], causal_f, of_ref, rf, h) for h in range(n_heads)]
                + [(heads_b[h], causal_b, ob_ref, rb, h) for h in range(n_heads)])
        atts = [_hgrn_scores_t(a["q_cat"], a["k_cat"], a["backward"]) for (a, _, _, _, _) in work]
        updates = [_dot(a["v"].T, a["k_hat"]) for (a, _, _, _, _) in work]
        atts = [jnp.where(causal, att, 0.0).astype(BF16).T
                for att, (_, causal, _, _, _) in zip(atts, work)]
        outs = [_dot(jnp.concatenate([a["q_hat"], att], axis=1),
                     jnp.concatenate([st.astype(BF16).T, a["v"]], axis=0))
                for att, st, (a, _, _, _, _) in zip(atts, states, work)]
        new_states = []
        for (a, _, o_ref, rows, h), st, out, upd in zip(work, states, outs, updates):
            o_ref[rows, h * HG_DK:(h + 1) * HG_DK] = out.astype(o_ref.dtype)
            decayed = st.reshape(HG_DK // SUB, SUB, HG_DK) * a["dec"][None]
            new_states.append(decayed.reshape(HG_DK, HG_DK) + upd)
        states = new_states
    for h in range(n_heads):
        stf_ref[h] = states[h]
        stb_ref[h] = states[n_heads + h]


def _hgrn(u, lb, kg, cb_nk, cast_weights=(), layer=0):
    s = u.shape[0]
    t = HG_CHUNKS_PER_STEP * CHUNK
    n = s // t
    assert s % t == 0 and NA_KT_TOKENS % t == 0 and t % 128 == 0
    per_tile = NA_KT_TOKENS // t
    fwd = lambda cb: pl.BlockSpec((t, HG_W), lambda i, cb=cb: (i, cb))
    bwd = lambda cb: pl.BlockSpec((t, HG_W), lambda i, cb=cb: (n - 1 - i, cb))
    heads = HG_W // HG_DK
    kg_spec, kg = _layer_row(kg, layer)
    cast_rows = [w.shape[1] // n for w in cast_weights]
    assert all(w.shape[1] == r * n and r % BF16_ROWS == 0 for w, r in zip(cast_weights, cast_rows))
    res = pl.pallas_call(
        functools.partial(_hgrn_kernel, n_cast=len(cast_weights)),
        grid=(n,),
        in_specs=[fwd(0), fwd(1), fwd(2), bwd(0), bwd(1), bwd(3),
                  pl.BlockSpec((None, 2, HG_W), lambda i: (layer, 0, 0)),
                  pl.BlockSpec((2, HG_SEL_ROWS, 2 * CHUNK), lambda i: (0, 0, 0)),
                  pl.BlockSpec((t, NA_W), lambda i: (i, cb_nk)),
                  kg_spec]
                 + [pl.BlockSpec((None, r, w.shape[2]), lambda i: (layer, i, 0))
                    for w, r in zip(cast_weights, cast_rows)],
        out_specs=[pl.BlockSpec((t, HG_W), lambda i: (i, 0)),
                   pl.BlockSpec((t, HG_W), lambda i: (n - 1 - i, 0)),
                   pl.BlockSpec((NA_HEADS, 1, NA_HD, t), lambda i: (0, i // per_tile, 0, i % per_tile))]
                  + [pl.BlockSpec((r, w.shape[2]), lambda i: (i, 0))
                     for w, r in zip(cast_weights, cast_rows)],
        out_shape=[jax.ShapeDtypeStruct((s, HG_W), BF16)] * 2
                  + [jax.ShapeDtypeStruct((NA_HEADS, s // NA_KT_TOKENS, NA_HD, NA_KT_TOKENS), BF16)]
                  + [jax.ShapeDtypeStruct(w.shape[1:], BF16) for w in cast_weights],
        scratch_shapes=[pltpu.VMEM((heads, HG_DK, HG_DK), F32)] * 2,
        compiler_params=pltpu.CompilerParams(
            dimension_semantics=("arbitrary",),
            vmem_limit_bytes=VMEM_LIMIT),
        name="hgrn",
    )(u, u, u, u, u, u, lb, jnp.asarray(_hgrn_selectors(), BF16), u, kg, *cast_weights)
    return res[0], res[1], res[2], list(res[3:])


NA_NEG_BLOCK = 2 * NA_KH - 1


def _natten_fill_bias(tz_ref, bias_ref, w0, r0, rows):
    kh = min(NA_KH, rows)
    lane = lax.broadcasted_iota(jnp.int32, (GRID_W, 2 * GRID_W), 1)

    def block_index(qi, a):
        r = r0 + qi
        r_start = min(max(r - kh // 2, 0), rows - kh)
        ar = w0 + a
        if r_start <= ar < r_start + kh:
            return ar - r + (NA_KH - 1)
        return NA_NEG_BLOCK

    for qi in range(NA_ROWS_PER_STEP):
        for ap in range(NA_WIN_ROWS // 2):
            ie, io = block_index(qi, 2 * ap), block_index(qi, 2 * ap + 1)
            blk = tz_ref[0, ie]
            if io != ie:
                blk = jnp.where(lane < GRID_W, blk, tz_ref[0, io])
            bias_ref[qi * GRID_W:(qi + 1) * GRID_W, ap * 2 * GRID_W:(ap + 1) * 2 * GRID_W] = blk


def _natten_kernel(q_ref, knt_ref, v_ref, gate_ref, qg_ref, tz_ref, o_ref, bias_ref):
    step = pl.program_id(1)
    s = v_ref.shape[0]
    rows = s // GRID_W
    rq, rk = NA_ROWS_PER_STEP, NA_WIN_ROWS
    n_blocks = rows // rq
    tq, win = rq * GRID_W, rk * GRID_W
    win_tiles = win // NA_KT_TOKENS

    @pl.when(step == 0)
    def _():
        _natten_fill_bias(tz_ref, bias_ref.at[0], 0, 0, rows)
        _natten_fill_bias(tz_ref, bias_ref.at[1], max(rq - NA_KH // 2, 0), rq, rows)
        _natten_fill_bias(tz_ref, bias_ref.at[2], rows - rk, rows - rq, rows)

    q = q_ref[...].astype(F32)
    ms = jnp.mean(q * q, axis=-1, keepdims=True)
    qn = (q * lax.rsqrt(ms + EPS) * (qg_ref[...] * (NA_HD ** -0.5 * LOG2E))).astype(BF16)

    qs, kws, vws, variants = [], [], [], []
    for c in range(NA_CHAINS):
        rb = step * NA_CHAINS + c
        w0 = jnp.clip(rq * rb - NA_KH // 2, 0, rows - rk) * GRID_W
        w0 = pl.multiple_of(w0, GRID_W)
        qs.append(qn[c * tq:(c + 1) * tq])
        kt = knt_ref[pl.ds(w0 // NA_KT_TOKENS, win_tiles)]
        kws.append(jnp.concatenate([kt[i] for i in range(win_tiles)], axis=1))
        vws.append(v_ref[pl.ds(w0, win), :])
        variants.append(jnp.where(rb == 0, 0, jnp.where(rb == n_blocks - 1, 2, 1)))
    scs = [_dot(qc, kw) for qc, kw in zip(qs, kws)]
    scs = [sc + bias_ref[var] for sc, var in zip(scs, variants)]
    ms = [jnp.max(sc, axis=-1, keepdims=True) for sc in scs]
    ps = [jnp.exp2(sc - m) for sc, m in zip(scs, ms)]
    ls = [jnp.sum(p, axis=-1, keepdims=True) for p in ps]
    outs = [_dot(p.astype(BF16), vw) for p, vw in zip(ps, vws)]
    for c, (o, l) in enumerate(zip(outs, ls)):
        gate = _silu(gate_ref[c * tq:(c + 1) * tq, :].astype(F32))
        o_ref[c * tq:(c + 1) * tq, :] = (o / l * gate).astype(o_ref.dtype)


def _natten_toeplitz(rpb):
    c = np.arange(GRID_W)
    c_start = np.clip(c - NA_KW // 2, 0, GRID_W - NA_KW)
    col_in = (c[None, :] >= c_start[:, None]) & (c[None, :] < c_start[:, None] + NA_KW)
    col_off = np.clip(c[None, :] - c[:, None] + (NA_KW - 1), 0, 2 * NA_KW - 2)
    sel = ((col_off[..., None] == np.arange(2 * NA_KW - 1)) & col_in[..., None]).astype(np.float32)
    sel = np.concatenate([sel, sel], axis=1) * LOG2E
    valid = np.concatenate([col_in, col_in], axis=1)[None] & (np.arange(2 * NA_KH) < NA_NEG_BLOCK)[:, None, None]
    neg = np.where(valid, 0.0, NEG).astype(np.float32)
    rpb = jnp.pad(rpb.astype(F32), ((0, 0), (0, 0), (0, 1), (0, 0)))
    return jnp.einsum('lhij,ckj->lhick', rpb, sel, precision=lax.Precision.HIGHEST) + neg


def _natten(u, knt, qg, tz, layer, col0):
    s = u.shape[0]
    blk = NA_ROWS_PER_STEP * GRID_W
    tq = NA_CHAINS * blk
    n_steps = s // tq
    assert s % tq == 0 and s // blk >= 3 and s // GRID_W >= NA_WIN_ROWS
    qg_spec, qg = _layer_row(qg, layer)
    assert blk % NA_KT_TOKENS == 0 and (NA_KH // 2 * GRID_W) % NA_KT_TOKENS == 0
    assert (NA_WIN_ROWS * GRID_W) % NA_KT_TOKENS == 0
    return pl.pallas_call(
        _natten_kernel,
        grid=(NA_HEADS, n_steps),
        in_specs=[
            pl.BlockSpec((tq, NA_HD), lambda h, rb: (rb, col0 + h)),
            pl.BlockSpec((None,) + knt.shape[1:], lambda h, rb: (h, 0, 0, 0)),
            pl.BlockSpec((s, NA_HD), lambda h, rb: (0, col0 + 2 * NA_HEADS + h)),
            pl.BlockSpec((tq, NA_HD), lambda h, rb: (rb, col0 + 3 * NA_HEADS + h)),
            qg_spec,
            pl.BlockSpec((None, 1) + tz.shape[2:], lambda h, rb: (layer, h, 0, 0, 0)),
        ],
        out_specs=pl.BlockSpec((tq, NA_HD), lambda h, rb: (rb, h)),
        out_shape=jax.ShapeDtypeStruct((s, NA_W), BF16),
        scratch_shapes=[pltpu.VMEM((3, blk, NA_WIN_ROWS * GRID_W), F32)],
        compiler_params=pltpu.CompilerParams(
            dimension_semantics=("parallel", "arbitrary"),
            vmem_limit_bytes=VMEM_LIMIT),
        name="natten",
    )(u, knt, u, u, qg, tz)


def _merge_kernel(of_ref, ob_ref, hg_ref, bin_ref, ma0_ref, ma1_ref, mb0_ref, mb1_ref,
                  x_ref, p_ref, on_ref, pn_ref, gn_ref, *rest):
    hbm_w, (o_ref, *maybe_hn_ref), vmem_w, sem = rest[:5], rest[5:-6], rest[-6:-1], rest[-1]
    wa_ref, wb_ref, wo_ref, wg_ref, wp_ref = vmem_w
    copies = [pltpu.make_async_copy(src, dst, sem.at[k])
              for k, (src, dst) in enumerate(zip(hbm_w, vmem_w))]
    first = pl.program_id(0) == 0

    @pl.when(first)
    def _():
        for cp in copies:
            cp.start()
        copies[0].wait()
        copies[1].wait()

    oa = of_ref[...].astype(F32) + ob_ref[...].astype(F32)
    gate_a = _silu(hg_ref[...].astype(F32))
    parts = []
    for h in range(HG_HEADS):
        hs = slice(h * HG_DK, (h + 1) * HG_DK)
        oh = oa[:, hs]
        ms = jnp.mean(oh * oh, axis=-1, keepdims=True)
        parts.append((oh * lax.rsqrt(ms + EPS) * on_ref[...] * gate_a[:, hs]).astype(BF16))
    a_in = jnp.concatenate(parts, axis=1)
    b_in = bin_ref[...]
    y_parts = []
    for t in range(0, wo_ref.shape[0], MERGE_TN):
        ya = _dot(a_in, wa_ref[:, t:t + MERGE_TN])
        yb = _dot(b_in, wb_ref[:, t:t + MERGE_TN])
        ma_ref = ma0_ref if t < HG_W else ma1_ref
        mb_ref = mb0_ref if t < HG_W else mb1_ref
        tt = t % HG_W
        m_a = ma_ref[:, tt:tt + MERGE_TN].astype(F32)
        m_b = mb_ref[:, tt:tt + MERGE_TN].astype(F32)
        y_parts.append((_sigmoid(m_a) * ya + _sigmoid(m_b) * yb).astype(BF16))
    @pl.when(first)
    def _():
        for cp in copies[2:]:
            cp.wait()

    x1 = x_ref[...] + _dot(jnp.concatenate(y_parts, axis=1), wo_ref[...])
    inv = lax.rsqrt(jnp.mean(x1 * x1, axis=-1, keepdims=True) + EPS)
    xg = (x1 * pn_ref[...]).astype(BF16)
    pb = p_ref[...].astype(BF16)
    x2_parts = []
    for t in range(0, wg_ref.shape[1], MERGE_TN):
        g = _sigmoid(_dot(xg, wg_ref[:, t:t + MERGE_TN]) * inv)
        x2_parts.append(x1[:, t:t + MERGE_TN] + _dot(pb, wp_ref[:, t:t + MERGE_TN]) * g)
    x2 = jnp.concatenate(x2_parts, axis=1)
    o_ref[...] = x2
    for hn_ref in maybe_hn_ref:
        hn_ref[...] = _rms_bf16(x2, gn_ref[...])


def _merge(o_f, o_b, u, b_in, x, p, p_index, onorm, pnorm, in_norm, layer, weights, cb_hgate, cb_ma,
           tm=MERGE_TM):
    s, d = x.shape
    tm = min(tm, s)
    pd = p.shape[-1]
    act = lambda wdt: pl.BlockSpec((tm, wdt), lambda i: (i, 0))
    ucol = lambda cb: pl.BlockSpec((tm, HG_W), lambda i, cb=cb: (i, cb))
    emit_next = layer + 1 < in_norm.shape[0]
    on_spec, onorm = _layer_row(onorm, layer)
    pn_spec, pnorm = _layer_row(pnorm, layer)
    gn_spec, in_norm = _layer_row(in_norm, layer + 1 if emit_next else layer)
    out_specs = [act(d)] + ([act(d)] if emit_next else [])
    out_shape = [jax.ShapeDtypeStruct((s, d), F32)] + (
        [jax.ShapeDtypeStruct((s, d), BF16)] if emit_next else [])
    res = pl.pallas_call(
        _merge_kernel,
        grid=(s // tm,),
        in_specs=[act(HG_W), act(HG_W), ucol(cb_hgate), act(NA_W),
                  ucol(cb_ma), ucol(cb_ma + 1), ucol(cb_ma + 2), ucol(cb_ma + 3),
                  act(d), pl.BlockSpec((None, tm, pd), lambda i: (p_index, i, 0)),
                  on_spec, pn_spec, gn_spec]
                 + [pl.BlockSpec(memory_space=pl.ANY) for _ in weights],
        out_specs=out_specs,
        out_shape=out_shape,
        scratch_shapes=[pltpu.VMEM(w.shape, w.dtype) for w in weights]
                       + [pltpu.SemaphoreType.DMA((len(weights),))],
        compiler_params=pltpu.CompilerParams(
            dimension_semantics=("arbitrary",),
            vmem_limit_bytes=VMEM_LIMIT),
        name="merge",
    )(o_f, o_b, u, b_in, u, u, u, u, x, p, onorm, pnorm, in_norm, *weights)
    return (res[0], res[1]) if emit_next else (res[0], None)


def kernel(x, p, norm_g, w_in, hgrn_lb, hgrn_onorm, na_qnorm, na_knorm, na_rpb,
           w_branch_a, w_branch_b, w_out, ple_norm, w_ple_gate, w_ple):
    bsz, s, d = x.shape
    depth = w_in.shape[0]
    lbp = jax.nn.softmax(hgrn_lb.astype(F32), axis=0)
    lower = jnp.cumsum(lbp, axis=0) - lbp[0:1]
    p_flat = p.reshape(depth * bsz, s, p.shape[-1])
    tz = _natten_toeplitz(na_rpb)
    w_ple_bf16 = w_ple.astype(BF16)
    cb_hgate = 4 * HG_W // HG_W
    cb_na = 5 * HG_W // NA_HD
    cb_ma = (5 * HG_W + 4 * NA_W) // HG_W

    outs = []
    for bi in range(bsz):
        xb = x[bi]
        h = _rmsnorm(xb, norm_g, 0)
        for i in range(depth):
            u = _in_proj(h, w_in, i)
            o_f, o_b, knt, weights = _hgrn(u, lower, na_knorm, (5 * HG_W + NA_W) // NA_W,
                                           (w_branch_a, w_branch_b, w_out, w_ple_gate), i)
            b_in = _natten(u, knt, na_qnorm, tz, i, col0=cb_na)
            xb, h = _merge(o_f, o_b, u, b_in, xb, p_flat, i * bsz + bi, hgrn_onorm, ple_norm, norm_g, i,
                           weights + [w_ple_bf16[i]], cb_hgate=cb_hgate, cb_ma=cb_ma)
        outs.append(xb)
    return jnp.stack(outs, axis=0)
```

```python
import functools

import jax
import jax.numpy as jnp
import numpy as np
from jax import lax
from jax.experimental import pallas as pl
from jax.experimental.pallas import tpu as pltpu

F32 = jnp.float32
BF16 = jnp.bfloat16

GRID_W = 64
HG_HEADS = 8
HG_DK = 128
HG_W = HG_HEADS * HG_DK
NA_HEADS = 8
NA_HD = 128
NA_W = NA_HEADS * NA_HD
NA_KH = 8
NA_KW = 16
EPS = 1e-6
NEG = -1e30
F_MIN = 1e-6

CHUNK = 64
HG_CHUNKS_PER_STEP = 4
SUB = 8
NSUB = CHUNK // SUB
MID = SUB // 2

NA_ROWS_PER_STEP = 4
NA_WIN_ROWS = 12
NA_CHAINS = 16
NA_KT_TOKENS = 256
LOG2E = 1.4426950408889634

VMEM_LIMIT = 56 * 1024 * 1024
BF16_ROWS = 16

RMSNORM_TM = 1024
IN_PROJ_TM = 2048
IN_PROJ_TN = 1024
MERGE_TM = 256
MERGE_TN = 512


def _dot(a, b):
    return jnp.dot(a, b, preferred_element_type=F32)


def _dot_nt(a, b):
    return lax.dot_general(a, b, (((1,), (1,)), ((), ())), preferred_element_type=F32)


def _sigmoid(x):
    return 1.0 / (1.0 + jnp.exp(-x))


def _silu(x):
    return x * _sigmoid(x)


def _layer_row(stack, layer):
    n = stack.shape[-1]
    return pl.BlockSpec((None, 1, n), lambda *_: (layer, 0, 0)), stack.reshape(stack.shape[0], 1, n)


def _rms_bf16(x, g):
    ms = jnp.mean(x * x, axis=-1, keepdims=True)
    return (x * lax.rsqrt(ms + EPS) * g).astype(BF16)


def _rmsnorm_kernel(x_ref, g_ref, o_ref):
    o_ref[...] = _rms_bf16(x_ref[...], g_ref[...])


def _rmsnorm(x, gains, layer, tm=RMSNORM_TM):
    s, d = x.shape
    tm = min(tm, s)
    g_spec, g = _layer_row(gains, layer)
    return pl.pallas_call(
        _rmsnorm_kernel,
        grid=(s // tm,),
        in_specs=[pl.BlockSpec((tm, d), lambda i: (i, 0)), g_spec],
        out_specs=pl.BlockSpec((tm, d), lambda i: (i, 0)),
        out_shape=jax.ShapeDtypeStruct((s, d), BF16),
        compiler_params=pltpu.CompilerParams(
            dimension_semantics=("parallel",),
            vmem_limit_bytes=VMEM_LIMIT),
        name="rmsnorm",
    )(x, g)


def _inproj_kernel(h_ref, w_ref, o_ref, wb_ref):
    @pl.when(pl.program_id(1) == 0)
    def _():
        wb_ref[...] = w_ref[...].astype(BF16)

    o_ref[...] = _dot(h_ref[...], wb_ref[...]).astype(o_ref.dtype)


def _in_proj(h, w, layer, tm=IN_PROJ_TM, tn=IN_PROJ_TN):
    s, d = h.shape
    n = w.shape[2]
    tm = min(tm, s)
    return pl.pallas_call(
        _inproj_kernel,
        grid=(n // tn, s // tm),
        in_specs=[
            pl.BlockSpec((tm, d), lambda j, i: (i, 0)),
            pl.BlockSpec((None, d, tn), lambda j, i: (layer, 0, j)),
        ],
        out_specs=pl.BlockSpec((tm, tn), lambda j, i: (i, j)),
        out_shape=jax.ShapeDtypeStruct((s, n), BF16),
        scratch_shapes=[pltpu.VMEM((d, tn), BF16)],
        compiler_params=pltpu.CompilerParams(
            dimension_semantics=("parallel", "arbitrary"),
            vmem_limit_bytes=VMEM_LIMIT),
        name="in_proj",
    )(h, w)


def _hgrn_selectors():
    c = CHUNK
    t = np.arange(c)[:, None]
    r = np.arange(c)[None, :]
    m = (t // SUB) * SUB + MID
    out = []
    for backward in (False, True):
        if backward:
            cum = lambda p: (r >= p).astype(np.float32)
            prev_m = m - SUB
        else:
            cum = lambda p: (r <= p).astype(np.float32)
            prev_m = m + SUB
        has_prev = (prev_m >= 0) & (prev_m < c)
        step = np.where(has_prev, cum(np.clip(prev_m, 0, c - 1)) - cum(m), 0.0)
        total = np.ones((BF16_ROWS, c), np.float32)
        first_m, last_m = (MID, c - SUB + MID) if backward else (c - SUB + MID, MID)
        at_last = np.broadcast_to(cum(last_m), (BF16_ROWS, c))
        past_first = np.broadcast_to(1.0 - cum(first_m), (BF16_ROWS, c))
        sel = np.concatenate([cum(t) - cum(m), step, at_last, past_first, total], axis=0)
        out.append(np.concatenate([sel, sel], axis=1))
    return np.stack(out)


HG_SEL_ROWS = 2 * CHUNK + 3 * BF16_ROWS


def _hgrn_prologue(q_ref, v_ref, z_ref, rows, lb, sel, backward):
    c = CHUNK
    w = q_ref.shape[1]
    q = q_ref[rows, :].astype(F32)
    v = v_ref[rows, :]
    z = z_ref[rows, :].astype(F32)

    f = lb + (1.0 - lb) * _sigmoid(z)
    g = jnp.log2(jnp.maximum(f, F_MIN))
    k = 1.0 - f
    qs = q * _sigmoid(q)

    hi = g.astype(BF16)
    lo = (g - hi.astype(F32)).astype(BF16)
    cums = _dot(sel, jnp.concatenate([hi, lo], axis=0))
    e = jnp.exp2(cums[0:c])
    step = jnp.exp2(cums[c:2 * c])
    tail = [jnp.exp2(cums[2 * c + i * BF16_ROWS:2 * c + i * BF16_ROWS + SUB]) for i in range(3)]
    last_to_state, end_to_first, dec = tail

    q0 = qs * e
    k0 = k * (1.0 / e)

    blocks = [slice(j * SUB, (j + 1) * SUB) for j in range(NSUB)]
    zero_blk = jnp.zeros((SUB, w), F32)
    order = range(NSUB) if backward else range(NSUB - 1, -1, -1)
    cur = [None] * NSUB
    q_tilde = [None] * NSUB
    k_tilde = [None] * NSUB
    k_hat_blocks = [None] * NSUB
    from_ref = None
    for j in order:
        for jj in range(NSUB):
            if cur[jj] is not None:
                cur[jj] = cur[jj] * step[blocks[j]]
        cur[j] = q0[blocks[j]]
        from_ref = end_to_first if from_ref is None else from_ref * step[blocks[j]]
        k_hat_blocks[j] = k0[blocks[j]] * from_ref
        q_tilde[j] = jnp.concatenate(
            [zero_blk if cur[jj] is None else cur[jj] for jj in range(NSUB)], axis=0).astype(BF16)

    q_hat = jnp.concatenate([blk * last_to_state for blk in cur], axis=0).astype(BF16)
    k_hat = jnp.concatenate(k_hat_blocks, axis=0).astype(BF16)

    k0b = k0.astype(BF16)
    row = lax.broadcasted_iota(jnp.int32, (c, c), 0)
    col = lax.broadcasted_iota(jnp.int32, (c, c), 1)
    causal = (row >= col) if backward else (row <= col)
    heads = []
    for h in range(w // HG_DK):
        hs = slice(h * HG_DK, (h + 1) * HG_DK)
        heads.append(dict(
            q_hat=q_hat[:, hs], k_hat=k_hat[:, hs], v=v[:, hs], dec=dec[:, hs], backward=backward,
            q_cat=jnp.concatenate([q_tilde[j][:, hs] for j in range(NSUB)], axis=1),
            k0=k0b[:, hs]))
    return heads, causal


def _hgrn_scores_t(q_cat, k0, backward):
    rt = 2 * SUB
    kw = 2 * HG_DK
    rows = []
    for p in range(CHUNK // rt):
        t_rows = slice(0, (p + 1) * rt) if backward else slice(p * rt, CHUNK)
        kp = k0[p * rt:(p + 1) * rt]
        top = lax.broadcasted_iota(jnp.int32, kp.shape, 0) < SUB
        zero = jnp.zeros_like(kp)
        kp = jnp.concatenate([jnp.where(top, kp, zero), jnp.where(top, zero, kp)], axis=1)
        part = _dot_nt(kp, q_cat[t_rows, p * kw:(p + 1) * kw])
        pad = CHUNK - part.shape[1]
        if pad:
            zero = jnp.zeros((rt, pad), F32)
            part = jnp.concatenate([part, zero] if backward else [zero, part], axis=1)
        rows.append(part)
    return jnp.concatenate(rows, axis=0)


def _hgrn_kernel(*refs, n_cast):
    (qf_ref, vf_ref, zf_ref, qb_ref, vb_ref, zb_ref, lb_ref, sel_ref, nk_ref, kg_ref) = refs[:10]
    refs = refs[10:]
    w32_refs, refs = refs[:n_cast], refs[n_cast:]
    (of_ref, ob_ref, knt_ref), refs = refs[:3], refs[3:]
    w16_refs, (stf_ref, stb_ref) = refs[:n_cast], refs[n_cast:]

    @pl.when(pl.program_id(0) == 0)
    def _():
        stf_ref[...] = jnp.zeros_like(stf_ref)
        stb_ref[...] = jnp.zeros_like(stb_ref)

    for w32_ref, w16_ref in zip(w32_refs, w16_refs):
        w16_ref[...] = w32_ref[...].astype(w16_ref.dtype)
    for h in range(knt_ref.shape[0]):
        kk = nk_ref[:, h * NA_HD:(h + 1) * NA_HD].astype(F32)
        knt_ref[h, 0] = _rms_bf16(kk, kg_ref[...]).T

    n_heads = stf_ref.shape[0]
    n_chunks = qf_ref.shape[0] // CHUNK
    rows_f = [slice(c * CHUNK, (c + 1) * CHUNK) for c in range(n_chunks)]
    rows_b = rows_f[::-1]
    pro = [(_hgrn_prologue(qf_ref, vf_ref, zf_ref, rf, lb_ref[0:1, :], sel_ref[0], False),
            _hgrn_prologue(qb_ref, vb_ref, zb_ref, rb, lb_ref[1:2, :], sel_ref[1], True))
           for rf, rb in zip(rows_f, rows_b)]

    states = [stf_ref[h] for h in range(n_heads)] + [stb_ref[h] for h in range(n_heads)]
    for (rf, rb), ((heads_f, causal_f), (heads_b, causal_b)) in zip(zip(rows_f, rows_b), pro):
        work = ([(heads_f[h], causal_f, of_ref, rf, h) for h in range(n_heads)]
                + [(heads_b[h], causal_b, ob_ref, rb, h) for h in range(n_heads)])
        atts = [_hgrn_scores_t(a["q_cat"], a["k0"], a["backward"]) for (a, _, _, _, _) in work]
        updates = [_dot(a["v"].T, a["k_hat"]) for (a, _, _, _, _) in work]
        atts = [jnp.where(causal, att, 0.0).astype(BF16).T
                for att, (_, causal, _, _, _) in zip(atts, work)]
        outs = [_dot(jnp.concatenate([a["q_hat"], att], axis=1),
                     jnp.concatenate([st.astype(BF16).T, a["v"]], axis=0))
                for att, st, (a, _, _, _, _) in zip(atts, states, work)]
        new_states = []
        for (a, _, o_ref, rows, h), st, out, upd in zip(work, states, outs, updates):
            o_ref[rows, h * HG_DK:(h + 1) * HG_DK] = out.astype(o_ref.dtype)
            decayed = st.reshape(HG_DK // SUB, SUB, HG_DK) * a["dec"][None]
            new_states.append(decayed.reshape(HG_DK, HG_DK) + upd)
        states = new_states
    for h in range(n_heads):
        stf_ref[h] = states[h]
        stb_ref[h] = states[n_heads + h]


def _hgrn(u, lb, kg, cb_nk, cast_weights=(), layer=0):
    s = u.shape[0]
    t = HG_CHUNKS_PER_STEP * CHUNK
    n = s // t
    assert s % t == 0 and NA_KT_TOKENS % t == 0 and t % 128 == 0
    per_tile = NA_KT_TOKENS // t
    fwd = lambda cb: pl.BlockSpec((t, HG_W), lambda i, cb=cb: (i, cb))
    bwd = lambda cb: pl.BlockSpec((t, HG_W), lambda i, cb=cb: (n - 1 - i, cb))
    heads = HG_W // HG_DK
    kg_spec, kg = _layer_row(kg, layer)
    cast_rows = [w.shape[1] // n for w in cast_weights]
    assert all(w.shape[1] == r * n and r % BF16_ROWS == 0 for w, r in zip(cast_weights, cast_rows))
    res = pl.pallas_call(
        functools.partial(_hgrn_kernel, n_cast=len(cast_weights)),
        grid=(n,),
        in_specs=[fwd(0), fwd(1), fwd(2), bwd(0), bwd(1), bwd(3),
                  pl.BlockSpec((None, 2, HG_W), lambda i: (layer, 0, 0)),
                  pl.BlockSpec((2, HG_SEL_ROWS, 2 * CHUNK), lambda i: (0, 0, 0)),
                  pl.BlockSpec((t, NA_W), lambda i: (i, cb_nk)),
                  kg_spec]
                 + [pl.BlockSpec((None, r, w.shape[2]), lambda i: (layer, i, 0))
                    for w, r in zip(cast_weights, cast_rows)],
        out_specs=[pl.BlockSpec((t, HG_W), lambda i: (i, 0)),
                   pl.BlockSpec((t, HG_W), lambda i: (n - 1 - i, 0)),
                   pl.BlockSpec((NA_HEADS, 1, NA_HD, t), lambda i: (0, i // per_tile, 0, i % per_tile))]
                  + [pl.BlockSpec((r, w.shape[2]), lambda i: (i, 0))
                     for w, r in zip(cast_weights, cast_rows)],
        out_shape=[jax.ShapeDtypeStruct((s, HG_W), BF16)] * 2
                  + [jax.ShapeDtypeStruct((NA_HEADS, s // NA_KT_TOKENS, NA_HD, NA_KT_TOKENS), BF16)]
                  + [jax.ShapeDtypeStruct(w.shape[1:], BF16) for w in cast_weights],
        scratch_shapes=[pltpu.VMEM((heads, HG_DK, HG_DK), F32)] * 2,
        compiler_params=pltpu.CompilerParams(
            dimension_semantics=("arbitrary",),
            vmem_limit_bytes=VMEM_LIMIT),
        name="hgrn",
    )(u, u, u, u, u, u, lb, jnp.asarray(_hgrn_selectors(), BF16), u, kg, *cast_weights)
    return res[0], res[1], res[2], list(res[3:])


NA_NEG_BLOCK = 2 * NA_KH - 1


def _natten_fill_bias(tz_ref, bias_ref, w0, r0, rows):
    kh = min(NA_KH, rows)
    lane = lax.broadcasted_iota(jnp.int32, (GRID_W, 2 * GRID_W), 1)

    def block_index(qi, a):
        r = r0 + qi
        r_start = min(max(r - kh // 2, 0), rows - kh)
        ar = w0 + a
        if r_start <= ar < r_start + kh:
            return ar - r + (NA_KH - 1)
        return NA_NEG_BLOCK

    for qi in range(NA_ROWS_PER_STEP):
        for ap in range(NA_WIN_ROWS // 2):
            ie, io = block_index(qi, 2 * ap), block_index(qi, 2 * ap + 1)
            blk = tz_ref[0, ie]
            if io != ie:
                blk = jnp.where(lane < GRID_W, blk, tz_ref[0, io])
            bias_ref[qi * GRID_W:(qi + 1) * GRID_W, ap * 2 * GRID_W:(ap + 1) * 2 * GRID_W] = blk


def _natten_kernel(q_ref, knt_ref, v_ref, gate_ref, qg_ref, tz_ref, o_ref, bias_ref):
    step = pl.program_id(1)
    s = v_ref.shape[0]
    rows = s // GRID_W
    rq, rk = NA_ROWS_PER_STEP, NA_WIN_ROWS
    n_blocks = rows // rq
    tq, win = rq * GRID_W, rk * GRID_W
    win_tiles = win // NA_KT_TOKENS

    @pl.when(step == 0)
    def _():
        _natten_fill_bias(tz_ref, bias_ref.at[0], 0, 0, rows)
        _natten_fill_bias(tz_ref, bias_ref.at[1], max(rq - NA_KH // 2, 0), rq, rows)
        _natten_fill_bias(tz_ref, bias_ref.at[2], rows - rk, rows - rq, rows)

    q = q_ref[...].astype(F32)
    ms = jnp.mean(q * q, axis=-1, keepdims=True)
    qn = (q * lax.rsqrt(ms + EPS) * (qg_ref[...] * (NA_HD ** -0.5 * LOG2E))).astype(BF16)

    qs, kws, vws, variants = [], [], [], []
    for c in range(NA_CHAINS):
        rb = step * NA_CHAINS + c
        w0 = jnp.clip(rq * rb - NA_KH // 2, 0, rows - rk) * GRID_W
        w0 = pl.multiple_of(w0, GRID_W)
        qs.append(qn[c * tq:(c + 1) * tq])
        kt = knt_ref[pl.ds(w0 // NA_KT_TOKENS, win_tiles)]
        kws.append(jnp.concatenate([kt[i] for i in range(win_tiles)], axis=1))
        vws.append(v_ref[pl.ds(w0, win), :])
        variants.append(jnp.where(rb == 0, 0, jnp.where(rb == n_blocks - 1, 2, 1)))
    scs = [_dot(qc, kw) for qc, kw in zip(qs, kws)]
    scs = [sc + bias_ref[var] for sc, var in zip(scs, variants)]
    ms = [jnp.max(sc, axis=-1, keepdims=True) for sc in scs]
    ps = [jnp.exp2(sc - m) for sc, m in zip(scs, ms)]
    ls = [jnp.sum(p, axis=-1, keepdims=True) for p in ps]
    outs = [_dot(p.astype(BF16), vw) for p, vw in zip(ps, vws)]
    for c, (o, l) in enumerate(zip(outs, ls)):
        gate = _silu(gate_ref[c * tq:(c + 1) * tq, :].astype(F32))
        o_ref[c * tq:(c + 1) * tq, :] = (o / l * gate).astype(o_ref.dtype)


def _natten_toeplitz(rpb):
    c = np.arange(GRID_W)
    c_start = np.clip(c - NA_KW // 2, 0, GRID_W - NA_KW)
    col_in = (c[None, :] >= c_start[:, None]) & (c[None, :] < c_start[:, None] + NA_KW)
    col_off = np.clip(c[None, :] - c[:, None] + (NA_KW - 1), 0, 2 * NA_KW - 2)
    sel = ((col_off[..., None] == np.arange(2 * NA_KW - 1)) & col_in[..., None]).astype(np.float32)
    sel = np.concatenate([sel, sel], axis=1) * LOG2E
    valid = np.concatenate([col_in, col_in], axis=1)[None] & (np.arange(2 * NA_KH) < NA_NEG_BLOCK)[:, None, None]
    neg = np.where(valid, 0.0, NEG).astype(np.float32)
    rpb = jnp.pad(rpb.astype(F32), ((0, 0), (0, 0), (0, 1), (0, 0)))
    return jnp.einsum('lhij,ckj->lhick', rpb, sel, precision=lax.Precision.HIGHEST) + neg


def _natten(u, knt, qg, tz, layer, col0):
    s = u.shape[0]
    blk = NA_ROWS_PER_STEP * GRID_W
    tq = NA_CHAINS * blk
    n_steps = s // tq
    assert s % tq == 0 and s // blk >= 3 and s // GRID_W >= NA_WIN_ROWS
    qg_spec, qg = _layer_row(qg, layer)
    assert blk % NA_KT_TOKENS == 0 and (NA_KH // 2 * GRID_W) % NA_KT_TOKENS == 0
    assert (NA_WIN_ROWS * GRID_W) % NA_KT_TOKENS == 0
    return pl.pallas_call(
        _natten_kernel,
        grid=(NA_HEADS, n_steps),
        in_specs=[
            pl.BlockSpec((tq, NA_HD), lambda h, rb: (rb, col0 + h)),
            pl.BlockSpec((None,) + knt.shape[1:], lambda h, rb: (h, 0, 0, 0)),
            pl.BlockSpec((s, NA_HD), lambda h, rb: (0, col0 + 2 * NA_HEADS + h)),
            pl.BlockSpec((tq, NA_HD), lambda h, rb: (rb, col0 + 3 * NA_HEADS + h)),
            qg_spec,
            pl.BlockSpec((None, 1) + tz.shape[2:], lambda h, rb: (layer, h, 0, 0, 0)),
        ],
        out_specs=pl.BlockSpec((tq, NA_HD), lambda h, rb: (rb, h)),
        out_shape=jax.ShapeDtypeStruct((s, NA_W), BF16),
        scratch_shapes=[pltpu.VMEM((3, blk, NA_WIN_ROWS * GRID_W), F32)],
        compiler_params=pltpu.CompilerParams(
            dimension_semantics=("parallel", "arbitrary"),
            vmem_limit_bytes=VMEM_LIMIT),
        name="natten",
    )(u, knt, u, u, qg, tz)


def _merge_kernel(of_ref, ob_ref, hg_ref, bin_ref, ma0_ref, ma1_ref, mb0_ref, mb1_ref,
                  x_ref, p_ref, on_ref, pn_ref, gn_ref, wa_ref, wb_ref, wo_ref, wg_ref, wp_ref,
                  o_ref, *maybe_hn_ref):
    oa = of_ref[...].astype(F32) + ob_ref[...].astype(F32)
    gate_a = _silu(hg_ref[...].astype(F32))
    parts = []
    for h in range(HG_HEADS):
        hs = slice(h * HG_DK, (h + 1) * HG_DK)
        oh = oa[:, hs]
        ms = jnp.mean(oh * oh, axis=-1, keepdims=True)
        parts.append((oh * lax.rsqrt(ms + EPS) * on_ref[...] * gate_a[:, hs]).astype(BF16))
    a_in = jnp.concatenate(parts, axis=1)
    b_in = bin_ref[...]
    y_parts = []
    for t in range(0, wo_ref.shape[0], MERGE_TN):
        ya = _dot(a_in, wa_ref[:, t:t + MERGE_TN])
        yb = _dot(b_in, wb_ref[:, t:t + MERGE_TN])
        ma_ref = ma0_ref if t < HG_W else ma1_ref
        mb_ref = mb0_ref if t < HG_W else mb1_ref
        tt = t % HG_W
        m_a = ma_ref[:, tt:tt + MERGE_TN].astype(F32)
        m_b = mb_ref[:, tt:tt + MERGE_TN].astype(F32)
        y_parts.append((_sigmoid(m_a) * ya + _sigmoid(m_b) * yb).astype(BF16))
    x1 = x_ref[...] + _dot(jnp.concatenate(y_parts, axis=1), wo_ref[...])
    inv = lax.rsqrt(jnp.mean(x1 * x1, axis=-1, keepdims=True) + EPS)
    xg = (x1 * pn_ref[...]).astype(BF16)
    pb = p_ref[...].astype(BF16)
    x2_parts = []
    for t in range(0, wg_ref.shape[1], MERGE_TN):
        g = _sigmoid(_dot(xg, wg_ref[:, t:t + MERGE_TN]) * inv)
        x2_parts.append(x1[:, t:t + MERGE_TN] + _dot(pb, wp_ref[:, t:t + MERGE_TN]) * g)
    x2 = jnp.concatenate(x2_parts, axis=1)
    o_ref[...] = x2
    for hn_ref in maybe_hn_ref:
        hn_ref[...] = _rms_bf16(x2, gn_ref[...])


def _merge(o_f, o_b, u, b_in, x, p, p_index, onorm, pnorm, in_norm, layer, weights, cb_hgate, cb_ma,
           tm=MERGE_TM):
    s, d = x.shape
    tm = min(tm, s)
    pd = p.shape[-1]
    act = lambda wdt: pl.BlockSpec((tm, wdt), lambda i: (i, 0))
    ucol = lambda cb: pl.BlockSpec((tm, HG_W), lambda i, cb=cb: (i, cb))
    const = lambda shp: pl.BlockSpec(shp, lambda i: (0, 0), pipeline_mode=pl.Buffered(1))
    emit_next = layer + 1 < in_norm.shape[0]
    on_spec, onorm = _layer_row(onorm, layer)
    pn_spec, pnorm = _layer_row(pnorm, layer)
    gn_spec, in_norm = _layer_row(in_norm, layer + 1 if emit_next else layer)
    out_specs = [act(d)] + ([act(d)] if emit_next else [])
    out_shape = [jax.ShapeDtypeStruct((s, d), F32)] + (
        [jax.ShapeDtypeStruct((s, d), BF16)] if emit_next else [])
    res = pl.pallas_call(
        _merge_kernel,
        grid=(s // tm,),
        in_specs=[act(HG_W), act(HG_W), ucol(cb_hgate), act(NA_W),
                  ucol(cb_ma), ucol(cb_ma + 1), ucol(cb_ma + 2), ucol(cb_ma + 3),
                  act(d), pl.BlockSpec((None, tm, pd), lambda i: (p_index, i, 0)),
                  on_spec, pn_spec, gn_spec]
                 + [const(w.shape) for w in weights],
        out_specs=out_specs,
        out_shape=out_shape,
        compiler_params=pltpu.CompilerParams(
            dimension_semantics=("parallel",),
            vmem_limit_bytes=VMEM_LIMIT),
        name="merge",
    )(o_f, o_b, u, b_in, u, u, u, u, x, p, onorm, pnorm, in_norm, *weights)
    return (res[0], res[1]) if emit_next else (res[0], None)


def kernel(x, p, norm_g, w_in, hgrn_lb, hgrn_onorm, na_qnorm, na_knorm, na_rpb,
           w_branch_a, w_branch_b, w_out, ple_norm, w_ple_gate, w_ple):
    bsz, s, d = x.shape
    depth = w_in.shape[0]
    lbp = jax.nn.softmax(hgrn_lb.astype(F32), axis=0)
    lower = jnp.cumsum(lbp, axis=0) - lbp[0:1]
    p_flat = p.reshape(depth * bsz, s, p.shape[-1])
    tz = _natten_toeplitz(na_rpb)
    w_ple_bf16 = w_ple.astype(BF16)
    cb_hgate = 4 * HG_W // HG_W
    cb_na = 5 * HG_W // NA_HD
    cb_ma = (5 * HG_W + 4 * NA_W) // HG_W

    outs = []
    for bi in range(bsz):
        xb = x[bi]
        h = _rmsnorm(xb, norm_g, 0)
        for i in range(depth):
            u = _in_proj(h, w_in, i)
            o_f, o_b, knt, weights = _hgrn(u, lower, na_knorm, (5 * HG_W + NA_W) // NA_W,
                                           (w_branch_a, w_branch_b, w_out, w_ple_gate), i)
            b_in = _natten(u, knt, na_qnorm, tz, i, col0=cb_na)
            xb, h = _merge(o_f, o_b, u, b_in, xb, p_flat, i * bsz + bi, hgrn_onorm, ple_norm, norm_g, i,
                           weights + [w_ple_bf16[i]], cb_hgate=cb_hgate, cb_ma=cb_ma)
        outs.append(xb)
    return jnp.stack(outs, axis=0)
```
